```python
import math
import jax, jax.numpy as jnp
from jax import lax
import numpy as np

D_MODEL = 4096
BATCH = 4
SEQ = 2048
DEPTH = 2
DEC_BATCH = 8
DEC_SEQ = 1
PAST_LEN = 16384
PAGE_SIZE = 128

N_A_LAYERS = DEPTH // 2
N_B_LAYERS = DEPTH - N_A_LAYERS
RET_HEADS = 16
RET_DK = D_MODEL // RET_HEADS
RET_DV = 2 * D_MODEL // RET_HEADS
RET_CHUNK = 128
ROPE_BASE = 10000.0
NSA_HEADS = 32
NSA_HEAD_DIM = D_MODEL // NSA_HEADS
NSA_KV_HEADS = 4
NSA_GROUP = NSA_HEADS // NSA_KV_HEADS
CMP_STRIDE = 16
CMP_LEN = 32
CMP_HIDDEN = 2 * NSA_HEAD_DIM
SEL_LEN = 64
SEL_TOP = 16
WINDOW = 512
Q_BLOCK = 32
N_KV_SLOTS = 6
D_FF = 4 * D_MODEL
REL_BUCKETS = 32
REL_MAX_DIST = 128
DN_ALPHA = (2.0 * DEPTH) ** 0.25
DN_BETA = (8.0 * DEPTH) ** -0.25
LN_EPS = 1e-5
GN_EPS = 1e-6
NEG_INF = -1e30

kernel_name = 'yoco_retention_nsa_decoder_step'


def _layernorm(x, g, b):
    xf = x.astype(jnp.float32)
    mu = xf.mean(-1, keepdims=True)
    var = jnp.square(xf - mu).mean(-1, keepdims=True)
    return ((xf - mu) * lax.rsqrt(var + LN_EPS) * g + b).astype(x.dtype)


def _sq_relu_mlp(x, w1, w2):
    return jnp.square(jax.nn.relu(x @ w1)) @ w2


def _rotary(a, pos):
    half = a.shape[-1] // 2
    inv = ROPE_BASE ** (-jnp.arange(half, dtype=jnp.float32) / half)
    ang = pos.astype(jnp.float32)[:, None] * inv[None, :]
    cos, sin = jnp.cos(ang)[:, None, :], jnp.sin(ang)[:, None, :]
    a1, a2 = a[..., :half].astype(jnp.float32), a[..., half:].astype(jnp.float32)
    return jnp.concatenate([a1 * cos - a2 * sin, a1 * sin + a2 * cos], -1).astype(a.dtype)


def _rel_bucket(dist):
    n = jnp.maximum(dist, 0)
    exact = REL_BUCKETS // 2
    nf = jnp.maximum(n, 1).astype(jnp.float32)
    large = exact + (jnp.log(nf / exact) / math.log(REL_MAX_DIST / exact) * (REL_BUCKETS - exact)).astype(jnp.int32)
    large = jnp.minimum(large, REL_BUCKETS - 1)
    return jnp.where(n < exact, n, large)


def _masked_softmax(s, mask):
    s = jnp.where(mask, s, NEG_INF)
    e = jnp.where(mask, jnp.exp(s - s.max(-1, keepdims=True)), 0.0)
    return e / jnp.maximum(e.sum(-1, keepdims=True), 1e-30)


def _retention(x, pos0, state0, w_in, gn_g, w_out):
    B, T, _ = x.shape
    C = min(RET_CHUNK, T)
    nc = T // C
    h = x @ w_in
    q, k, v, g = jnp.split(h, [D_MODEL, 2 * D_MODEL, 4 * D_MODEL], axis=-1)
    pos = pos0 + jnp.arange(T)
    q = _rotary(q.reshape(B, T, RET_HEADS, RET_DK), pos)
    k = _rotary(k.reshape(B, T, RET_HEADS, RET_DK), pos) * (RET_DK ** -0.5)
    v = v.reshape(B, T, RET_HEADS, RET_DV)

    def chunks(a):
        return a.reshape(B, nc, C, RET_HEADS, -1).transpose(1, 0, 3, 2, 4).astype(jnp.float32)

    lg = jnp.log1p(-jnp.exp2(-5.0 - jnp.arange(RET_HEADS, dtype=jnp.float32)))
    idx = jnp.arange(C, dtype=jnp.float32)
    diff = idx[:, None] - idx[None, :]
    dmask = jnp.where(diff >= 0, jnp.exp(jnp.maximum(diff, 0.0)[None] * lg[:, None, None]), 0.0)
    xi = jnp.exp((idx + 1.0)[None, :] * lg[:, None])
    zeta = jnp.exp((C - 1.0 - idx)[None, :] * lg[:, None])
    g_c = jnp.exp(C * lg)

    def step(S, qkv):
        qc, kc, vc = qkv
        att = jnp.einsum('bhnd,bhmd->bhnm', qc, kc) * dmask
        o = jnp.einsum('bhnm,bhme->bhne', att, vc) + jnp.einsum('bhnd,bhde->bhne', qc * xi[None, :, :, None], S)
        S = S * g_c[None, :, None, None] + jnp.einsum('bhmd,bhme->bhde', kc * zeta[None, :, :, None], vc)
        return S, o

    S, o = lax.scan(step, state0.astype(jnp.float32), (chunks(q), chunks(k), chunks(v)))
    o = o.transpose(1, 0, 3, 2, 4).reshape(B, T, RET_HEADS, RET_DV)
    mu = o.mean(-1, keepdims=True)
    var = jnp.square(o - mu).mean(-1, keepdims=True)
    o = ((o - mu) * lax.rsqrt(var + GN_EPS)).reshape(B, T, 2 * D_MODEL) * gn_g
    y = (jax.nn.silu(g.astype(jnp.float32)) * o).astype(x.dtype) @ w_out
    return y, S.astype(x.dtype)


def _compress(rows, pos_emb, w1, w2):
    B, T, G, dh = rows.shape
    m = CMP_LEN // CMP_STRIDE
    n_sub = T // CMP_STRIDE
    nc = n_sub - m + 1
    sub = rows[:, :n_sub * CMP_STRIDE].reshape(B, n_sub, CMP_STRIDE, G, dh)
    blocks = jnp.concatenate([sub[:, i:i + nc] for i in range(m)], axis=2)
    blocks = blocks + pos_emb[None, None, :, None, :]
    flat = blocks.transpose(0, 1, 3, 2, 4).reshape(B, nc, G, CMP_LEN * dh)
    return jax.nn.gelu(flat @ w1) @ w2


def _shared_kv(rows4, win2, cmp_pos, w_cmp1, w_cmp2):
    kc = _compress(rows4[:, :, 0], cmp_pos[0], w_cmp1[0], w_cmp2[0])
    vc = _compress(rows4[:, :, 1], cmp_pos[1], w_cmp1[1], w_cmp2[1])
    return kc, vc, rows4[:, :, 2], rows4[:, :, 3], win2[:, :, 0], win2[:, :, 1]


def _nsa(x, q_start, kc, vc, ks, vs, kw, vw, w_start, w_qg, w_o, rel_bias):
    B, T, _ = x.shape
    G, R, dh = NSA_KV_HEADS, NSA_GROUP, NSA_HEAD_DIM
    qg = x @ w_qg
    q = qg[..., :NSA_HEADS * dh].reshape(B, T, G, R, dh) * (dh ** -0.5)
    gates = jax.nn.sigmoid(qg[..., NSA_HEADS * dh:].astype(jnp.float32)).reshape(B, T, G, R, 3)
    QB = min(Q_BLOCK, T)
    nq = T // QB
    q_blocks = q.reshape(B, nq, QB, G, R, dh).swapaxes(0, 1)
    g_blocks = gates.reshape(B, nq, QB, G, R, 3).swapaxes(0, 1)
    starts = q_start + QB * jnp.arange(nq, dtype=jnp.int32)

    nc = kc.shape[1]
    c_end = jnp.arange(nc) * CMP_STRIDE + CMP_LEN - 1
    t_kv = ks.shape[1]
    nblk = -(-t_kv // SEL_LEN)
    pad = nblk * SEL_LEN - t_kv

    def to_blocks(a):
        return jnp.pad(a, ((0, 0), (0, pad), (0, 0), (0, 0))).reshape(B, nblk, SEL_LEN, G, dh).transpose(0, 3, 1, 2, 4)

    ksb, vsb = to_blocks(ks), to_blocks(vs)
    n_top = min(SEL_TOP, nblk)
    ratio = SEL_LEN // CMP_STRIDE
    w_imp = np.convolve(np.ones(ratio), np.ones(CMP_LEN // CMP_STRIDE))
    w_lo = jnp.asarray(w_imp[:ratio], jnp.float32)
    w_hi = jnp.asarray(np.pad(w_imp[ratio:], (0, 2 * ratio - len(w_imp))), jnp.float32)
    n_pad_c = ratio * (nblk + 1) - nc
    kwp = jnp.pad(kw, ((0, 0), (WINDOW, 0), (0, 0), (0, 0)))
    vwp = jnp.pad(vw, ((0, 0), (WINDOW, 0), (0, 0), (0, 0)))
    wk = WINDOW + QB
    tbl_g = rel_bias.reshape(REL_BUCKETS, G, R).transpose(1, 0, 2)
    g_ar = jnp.arange(G)[None, None, :, None, None]
    b_ar = jnp.arange(B)[:, None, None]
    gi_ar = jnp.arange(G)[None, :, None]

    def head_bias(dist):
        return rel_bias[_rel_bucket(dist)].reshape(dist.shape + (G, R)).transpose(0, 2, 3, 1)

    def one_block(args):
        qb, gb, qs = args
        t = qs + jnp.arange(QB)
        dist_c = t[:, None] - c_end[None, :]
        s_c = jnp.einsum('bqgrd,bcgd->bqgrc', qb, kc).astype(jnp.float32) + head_bias(dist_c)
        p_c = _masked_softmax(s_c, (dist_c >= 0)[:, None, None, :])
        o_c = jnp.einsum('bqgrc,bcgd->bqgrd', p_c.astype(vc.dtype), vc)
        imp = jnp.pad(p_c.sum(3), ((0, 0), (0, 0), (0, 0), (0, n_pad_c))).reshape(B, QB, G, nblk + 1, ratio)
        imp = imp[..., :-1, :] @ w_lo + imp[..., 1:, :] @ w_hi
        jb = jnp.arange(nblk)
        cur = (t // SEL_LEN)[:, None]
        forced = (jb == 0) | (jb == cur) | (jb == cur - 1)
        elig = jb <= cur
        score = jnp.where(forced[None, :, None, :], jnp.inf, jnp.where(elig[None, :, None, :], imp, -jnp.inf))
        _, sel = lax.top_k(score, n_top)
        sel_bg = sel.transpose(0, 2, 1, 3).reshape(B, G, QB * n_top)
        k_sel = ksb[b_ar, gi_ar, sel_bg].reshape(B, G, QB, n_top, SEL_LEN, dh)
        v_sel = vsb[b_ar, gi_ar, sel_bg].reshape(B, G, QB, n_top, SEL_LEN, dh)
        kpos = sel[..., None] * SEL_LEN + jnp.arange(SEL_LEN)
        dist_s = t[None, :, None, None, None] - kpos
        bias_s = jnp.moveaxis(tbl_g[g_ar, _rel_bucket(dist_s)], -1, 3)
        s_s = jnp.einsum('bqgrd,bgqnld->bqgrnl', qb, k_sel).astype(jnp.float32) + bias_s
        p_s = _masked_softmax(s_s.reshape(B, QB, G, R, n_top * SEL_LEN),
                              (dist_s >= 0).reshape(B, QB, G, 1, n_top * SEL_LEN))
        o_s = jnp.einsum('bqgrnl,bgqnld->bqgrd', p_s.reshape(s_s.shape).astype(vs.dtype), v_sel)
        off = qs - w_start
        k_w = lax.dynamic_slice_in_dim(kwp, off, wk, axis=1)
        v_w = lax.dynamic_slice_in_dim(vwp, off, wk, axis=1)
        wpos = qs - WINDOW + jnp.arange(wk)
        dist_w = t[:, None] - wpos[None, :]
        mask_w = (dist_w >= 0) & (dist_w <= WINDOW) & (wpos >= w_start)[None, :]
        s_w = jnp.einsum('bqgrd,bkgd->bqgrk', qb, k_w).astype(jnp.float32) + head_bias(dist_w)
        p_w = _masked_softmax(s_w, mask_w[:, None, None, :])
        o_w = jnp.einsum('bqgrk,bkgd->bqgrd', p_w.astype(vw.dtype), v_w)
        o = gb[..., 0:1] * o_c + gb[..., 1:2] * o_s + gb[..., 2:3] * o_w
        return o.astype(x.dtype)

    o = lax.map(one_block, (q_blocks, g_blocks, starts))
    o = o.swapaxes(0, 1).reshape(B, T, NSA_HEADS * dh)
    return o @ w_o


def _trunk(x, pos0, ret_state0, kv_fn, w_ret_in, ret_gn_g, w_ret_out, w_qg, w_nsa_out, rel_bias, w_ff1, w_ff2, ln_g, ln_b):
    ret_states = []
    for layer in range(DEPTH):
        if layer < N_A_LAYERS:
            a, st = _retention(x, pos0, ret_state0[layer], w_ret_in[layer], ret_gn_g[layer], w_ret_out[layer])
            ret_states.append(st)
        else:
            if layer == N_A_LAYERS:
                kc, vc, ks, vs, kw, vw, w_start, new_rows = kv_fn(x)
            j = layer - N_A_LAYERS
            a = _nsa(x, pos0, kc, vc, ks, vs, kw, vw, w_start, w_qg[j], w_nsa_out[j], rel_bias)
        x = _layernorm(DN_ALPHA * x + a, ln_g[layer, 0], ln_b[layer, 0])
        x = _layernorm(DN_ALPHA * x + _sq_relu_mlp(x, w_ff1[layer], w_ff2[layer]), ln_g[layer, 1], ln_b[layer, 1])
    return x, jnp.stack(ret_states), new_rows


def setup_inputs(seed: int = 0) -> dict:
    key = jax.random.key(seed)
    ks = jax.random.split(key, 20)
    f32 = jnp.float32
    n_pages = PAST_LEN // PAGE_SIZE
    n_used = DEC_BATCH * n_pages
    n_pool = n_used + (n_used + 3) // 4

    def nrm(k, shape, scale):
        return jax.random.normal(k, shape, f32) * scale

    hkv = NSA_KV_HEADS * NSA_HEAD_DIM
    return {
        'x_prompt': nrm(ks[0], (BATCH, SEQ, D_MODEL), 1.0),
        'x_sample': nrm(ks[1], (DEC_BATCH, DEC_SEQ, D_MODEL), 1.0),
        'state_ret': nrm(ks[2], (N_A_LAYERS, DEC_BATCH, RET_HEADS, RET_DK, RET_DV), RET_DK ** -0.5),
        'cache_kv': nrm(ks[3], (n_pool, PAGE_SIZE, 4, NSA_KV_HEADS, NSA_HEAD_DIM), 1.0),
        'cache_win': nrm(ks[4], (DEC_BATCH, min(WINDOW, PAST_LEN), 2, NSA_KV_HEADS, NSA_HEAD_DIM), 1.0),
        'page_table': jax.random.permutation(ks[5], n_pool)[:n_used].reshape(DEC_BATCH, n_pages).astype(jnp.int32),
        'w_ret_in': nrm(ks[6], (N_A_LAYERS, D_MODEL, 6 * D_MODEL), D_MODEL ** -0.5),
        'ret_gn_g': 1.0 + nrm(ks[7], (N_A_LAYERS, 2 * D_MODEL), 0.02),
        'w_ret_out': nrm(ks[8], (N_A_LAYERS, 2 * D_MODEL, D_MODEL), (2 * D_MODEL) ** -0.5 * DN_BETA),
        'w_kv': nrm(ks[9], (D_MODEL, N_KV_SLOTS * hkv), D_MODEL ** -0.5),
        'cmp_pos': nrm(ks[10], (2, CMP_LEN, NSA_HEAD_DIM), 0.1),
        'w_cmp1': nrm(ks[11], (2, CMP_LEN * NSA_HEAD_DIM, CMP_HIDDEN), (CMP_LEN * NSA_HEAD_DIM) ** -0.5),
        'w_cmp2': nrm(ks[12], (2, CMP_HIDDEN, NSA_HEAD_DIM), CMP_HIDDEN ** -0.5),
        'w_qg': nrm(ks[13], (N_B_LAYERS, D_MODEL, NSA_HEADS * NSA_HEAD_DIM + 3 * NSA_HEADS), D_MODEL ** -0.5),
        'w_nsa_out': nrm(ks[14], (N_B_LAYERS, NSA_HEADS * NSA_HEAD_DIM, D_MODEL), (NSA_HEADS * NSA_HEAD_DIM) ** -0.5 * DN_BETA),
        'rel_bias': nrm(ks[15], (REL_BUCKETS, NSA_HEADS), 0.5),
        'w_ff1': nrm(ks[16], (DEPTH, D_MODEL, D_FF), D_MODEL ** -0.5),
        'w_ff2': nrm(ks[17], (DEPTH, D_FF, D_MODEL), D_FF ** -0.5 * DN_BETA),
        'ln_g': 1.0 + nrm(ks[18], (DEPTH, 2, D_MODEL), 0.02),
        'ln_b': nrm(ks[19], (DEPTH, 2, D_MODEL), 0.02),
    }


def reference(x_prompt, x_sample, state_ret, cache_kv, cache_win, page_table, w_ret_in, ret_gn_g, w_ret_out, w_kv,
              cmp_pos, w_cmp1, w_cmp2, w_qg, w_nsa_out, rel_bias, w_ff1, w_ff2, ln_g, ln_b):
    n_seq, n_pages = page_table.shape
    past_len = n_pages * cache_kv.shape[1]

    def kv_proj(h):
        B, T, _ = h.shape
        return (h @ w_kv).reshape(B, T, N_KV_SLOTS, NSA_KV_HEADS, NSA_HEAD_DIM)

    def kv_prompt(h):
        kv = kv_proj(h)
        wp = min(WINDOW, h.shape[1])
        rows4, win2 = kv[:, :, :4], kv[:, :, 4:]
        kc, vc, ks, vs, kw, vw = _shared_kv(rows4, win2, cmp_pos, w_cmp1, w_cmp2)
        return kc, vc, ks, vs, kw, vw, 0, (rows4, win2[:, -wp:])

    def kv_sample(h):
        kv = kv_proj(h)
        past = cache_kv[page_table].reshape(n_seq, past_len, 4, NSA_KV_HEADS, NSA_HEAD_DIM)
        rows4 = jnp.concatenate([past, kv[:, :, :4]], axis=1)
        win2 = jnp.concatenate([cache_win, kv[:, :, 4:]], axis=1)
        kc, vc, ks, vs, kw, vw = _shared_kv(rows4, win2, cmp_pos, w_cmp1, w_cmp2)
        return kc, vc, ks, vs, kw, vw, past_len - cache_win.shape[1], (kv[:, :, :4], kv[:, :, 4:])

    zeros_state = jnp.zeros((N_A_LAYERS, x_prompt.shape[0], RET_HEADS, RET_DK, RET_DV), x_prompt.dtype)
    y_prompt, ret_state_prompt, (kv_rows_prompt, win_rows_prompt) = _trunk(
        x_prompt, 0, zeros_state, kv_prompt, w_ret_in, ret_gn_g, w_ret_out, w_qg, w_nsa_out, rel_bias,
        w_ff1, w_ff2, ln_g, ln_b)
    y_sample, ret_state_sample, (kv_rows_sample, win_rows_sample) = _trunk(
        x_sample, past_len, state_ret, kv_sample, w_ret_in, ret_gn_g, w_ret_out, w_qg, w_nsa_out, rel_bias,
        w_ff1, w_ff2, ln_g, ln_b)
    return (y_prompt, y_sample, ret_state_prompt, kv_rows_prompt, win_rows_prompt, ret_state_sample, kv_rows_sample, win_rows_sample)
```

```python
import functools
import math

import jax
import jax.numpy as jnp
import numpy as np
from jax import lax
from jax.experimental import pallas as pl
from jax.experimental.pallas import tpu as pltpu

RET_CHUNK = 128
ROPE_BASE = 10000.0
CMP_STRIDE = 16
CMP_LEN = 32
SEL_LEN = 64
SEL_TOP = 16
WINDOW = 512
REL_BUCKETS = 32
REL_MAX_DIST = 128
LN_EPS = 1e-5
GN_EPS = 1e-6
NEG_INF = -1e30

LANES = 128
SUBLANES = 8
VMEM_LIMIT_BYTES = 56 * 1024 * 1024

F32 = jnp.float32
BF16 = jnp.bfloat16
KEY_TILE = LANES


def _shr(x, pow2):
    assert pow2 & (pow2 - 1) == 0
    return x >> (pow2.bit_length() - 1)


def _params(*sem):
    return pltpu.CompilerParams(dimension_semantics=sem, vmem_limit_bytes=VMEM_LIMIT_BYTES)


def _ep_store(acc, o_ref, j):
    o_ref[...] = acc.astype(o_ref.dtype)


def _ep_relu2(acc, o_ref, j):
    r = jnp.maximum(acc, 0.0)
    o_ref[...] = (r * r).astype(o_ref.dtype)


def _ep_scale(acc, o_ref, j, *, scale):
    o_ref[...] = (acc * scale).astype(o_ref.dtype)


def _ep_sigmoid(acc, o_ref, j):
    o_ref[...] = jax.nn.sigmoid(acc).astype(o_ref.dtype)


def _ep_rotary(acc, o_ref, j, cos_ref, sin_ref, *, head_dim, k_col_start, k_scale):
    tn = acc.shape[1]
    half = head_dim // 2
    cos = cos_ref[...]
    sin = sin_ref[...]
    scale = jnp.where(j * tn >= k_col_start, k_scale, 1.0).astype(F32)
    for c in range(tn // head_dim):
        a1 = acc[:, c * head_dim:c * head_dim + half]
        a2 = acc[:, c * head_dim + half:(c + 1) * head_dim]
        o_ref[:, c * head_dim:c * head_dim + half] = ((a1 * cos - a2 * sin) * scale).astype(o_ref.dtype)
        o_ref[:, c * head_dim + half:(c + 1) * head_dim] = ((a1 * sin + a2 * cos) * scale).astype(o_ref.dtype)


def _mm_body(*refs, nk, n_extra, epilogue):
    x_ref, w_ref = refs[0], refs[1]
    extra = refs[2:2 + n_extra]
    o_ref = refs[2 + n_extra]
    j = pl.program_id(1)
    part = jnp.dot(x_ref[...].astype(BF16), w_ref[...].astype(BF16), preferred_element_type=F32)
    if nk == 1:
        epilogue(part, o_ref, j, *extra)
        return
    acc_ref = refs[3 + n_extra]
    k = pl.program_id(2)

    @pl.when(k == 0)
    def _():
        acc_ref[...] = part

    @pl.when((k > 0) & (k < nk - 1))
    def _():
        acc_ref[...] += part

    @pl.when(k == nk - 1)
    def _():
        epilogue(acc_ref[...] + part, o_ref, j, *extra)


def _pick(n, pref):
    t = min(n, pref)
    while n % t:
        t //= 2
    return t


def _matmul(x, w, n_out, *, col_off=0, out_dtype=F32, epilogue=_ep_store, extra=(), extra_specs=(),
            tm=2048, tn=1024, tk=1024, name="matmul"):
    m, kdim = x.shape
    tm = _pick(m, tm)
    tn = _pick(n_out, tn)
    tk = _pick(kdim, tk)
    assert col_off % tn == 0
    nk = kdim // tk
    joff = col_off // tn
    scratch = [pltpu.VMEM((tm, tn), F32)] if nk > 1 else []
    return pl.pallas_call(
        functools.partial(_mm_body, nk=nk, n_extra=len(extra), epilogue=epilogue),
        grid=(m // tm, n_out // tn, nk),
        in_specs=[pl.BlockSpec((tm, tk), lambda i, j, k: (i, k)),
                  pl.BlockSpec((tk, tn), lambda i, j, k: (k, j + joff))] + list(extra_specs),
        out_specs=pl.BlockSpec((tm, tn), lambda i, j, k: (i, j)),
        out_shape=jax.ShapeDtypeStruct((m, n_out), out_dtype),
        scratch_shapes=scratch,
        compiler_params=_params("parallel", "parallel", "arbitrary"),
        name=name,
    )(x, w, *extra)


def _ln_body(x_ref, a_ref, g_ref, b_ref, *o_refs, alpha):
    z = alpha * x_ref[...] + a_ref[...]
    mu = jnp.mean(z, axis=-1, keepdims=True)
    zc = z - mu
    var = jnp.mean(zc * zc, axis=-1, keepdims=True)
    y = zc * lax.rsqrt(var + LN_EPS) * g_ref[...] + b_ref[...]
    for o_ref in o_refs:
        o_ref[...] = y.astype(o_ref.dtype)


def _add_ln(x, a, g, b, alpha, *, want_bf16=True, name="add_ln"):
    m, d = x.shape
    tm = _pick(m, 256)
    row = pl.BlockSpec((tm, d), lambda i: (i, 0))
    vec = pl.BlockSpec((1, d), lambda i: (0, 0))
    shapes = [jax.ShapeDtypeStruct((m, d), F32)]
    if want_bf16:
        shapes.append(jax.ShapeDtypeStruct((m, d), BF16))
    outs = pl.pallas_call(
        functools.partial(_ln_body, alpha=alpha),
        grid=(m // tm,),
        in_specs=[row, row, vec, vec],
        out_specs=[row] * len(shapes),
        out_shape=shapes,
        compiler_params=_params("parallel"),
        name=name,
    )(x, a, g.reshape(1, d), b.reshape(1, d))
    return outs if want_bf16 else (outs[0], None)


def _group_norm_gate(o, gate, gn_g):
    mu = jnp.mean(o, axis=-1, keepdims=True)
    oc = o - mu
    var = jnp.mean(oc * oc, axis=-1, keepdims=True)
    return jax.nn.silu(gate) * (oc * lax.rsqrt(var + GN_EPS) * gn_g)


def _ret_chunk_body(q_ref, k_ref, v_ref, g_ref, gn_ref, lg_ref, o_ref, s_out_ref, s_ref, *, nc, chunk):
    c = pl.program_id(2)
    dk = q_ref.shape[2]
    dv = v_ref.shape[2]

    @pl.when(c == 0)
    def _():
        s_ref[...] = jnp.zeros_like(s_ref)

    lg = lg_ref[0]
    q = q_ref[0]
    k = k_ref[0]
    v = v_ref[0]
    row = lax.broadcasted_iota(jnp.int32, (chunk, chunk), 0)
    col = lax.broadcasted_iota(jnp.int32, (chunk, chunk), 1)
    diff = (row - col).astype(F32)
    dmask = jnp.where(diff >= 0, jnp.exp(jnp.maximum(diff, 0.0) * lg), 0.0)
    ridx = lax.broadcasted_iota(jnp.int32, (chunk, LANES), 0).astype(F32)
    xi = jnp.exp((ridx + 1.0) * lg)
    zeta = jnp.exp((chunk - 1.0 - ridx) * lg)
    g_c = jnp.exp(chunk * lg)

    att = lax.dot_general(q, k, (((1,), (1,)), ((), ())), preferred_element_type=F32) * dmask
    s_old = s_ref[...]
    cross = jnp.dot(q, s_old.astype(BF16), preferred_element_type=F32)
    o = jnp.dot(att.astype(BF16), v, preferred_element_type=F32) + cross * jnp.tile(xi, (1, dv // LANES))
    kz = (k.astype(F32) * jnp.tile(zeta, (1, dk // LANES))).astype(BF16)
    upd = lax.dot_general(kz, v, (((0,), (0,)), ((), ())), preferred_element_type=F32)
    s_new = s_old * jnp.tile(g_c, (1, dv // LANES)) + upd
    s_ref[...] = s_new
    o_ref[0] = _group_norm_gate(o, g_ref[0], gn_ref[...]).astype(o_ref.dtype)

    @pl.when(c == nc - 1)
    def _():
        s_out_ref[0, 0] = s_new


def _retention_prompt(qk, v, gate, gn_g, lg_tab, *, heads, dk, dv):
    b, t, _ = qk.shape
    chunk = min(RET_CHUNK, t)
    nc = t // chunk
    return pl.pallas_call(
        functools.partial(_ret_chunk_body, nc=nc, chunk=chunk),
        grid=(b, heads, nc),
        in_specs=[pl.BlockSpec((1, chunk, dk), lambda bi, h, c: (bi, c, h)),
                  pl.BlockSpec((1, chunk, dk), lambda bi, h, c: (bi, c, heads + h)),
                  pl.BlockSpec((1, chunk, dv), lambda bi, h, c: (bi, c, h)),
                  pl.BlockSpec((1, chunk, dv), lambda bi, h, c: (bi, c, h)),
                  pl.BlockSpec((1, dv), lambda bi, h, c: (0, h)),
                  pl.BlockSpec((1, 1, LANES), lambda bi, h, c: (h, 0, 0))],
        out_specs=[pl.BlockSpec((1, chunk, dv), lambda bi, h, c: (bi, c, h)),
                   pl.BlockSpec((1, 1, dk, dv), lambda bi, h, c: (bi, h, 0, 0))],
        out_shape=[jax.ShapeDtypeStruct((b, t, heads * dv), BF16),
                   jax.ShapeDtypeStruct((b, heads, dk, dv), F32)],
        scratch_shapes=[pltpu.VMEM((dk, dv), F32)],
        compiler_params=_params("parallel", "parallel", "arbitrary"),
        name="retention_chunks",
    )(qk, qk, v, gate, gn_g, lg_tab)


def _ret_step_body(q_ref, k_ref, v_ref, g_ref, gn_ref, lg_ref, s_ref, o_ref, s_out_ref):
    dk = q_ref.shape[2]
    dv = v_ref.shape[2]
    lg = lg_ref[0]
    gamma = jnp.exp(lg)
    q = q_ref[0].astype(BF16)
    k = k_ref[0].astype(BF16)
    v = v_ref[0].astype(BF16)
    s_old = s_ref[0, 0]
    att = jnp.sum(q.astype(F32) * k.astype(F32), axis=-1, keepdims=True)
    q8 = jnp.broadcast_to(q, (SUBLANES * 2, dk))
    cross = jnp.dot(q8, s_old.astype(BF16), preferred_element_type=F32)[0:1]
    o = att.astype(BF16).astype(F32) * v.astype(F32) + cross * jnp.tile(gamma, (1, dv // LANES))
    o_ref[0] = _group_norm_gate(o, g_ref[0], gn_ref[...]).astype(o_ref.dtype)
    rows = lax.broadcasted_iota(jnp.int32, (LANES, dk), 0)
    k_pad = jnp.where(rows == 0, jnp.broadcast_to(k.astype(F32), (LANES, dk)), 0.0).astype(BF16)
    v_pad = jnp.broadcast_to(v, (LANES, dv))
    upd = lax.dot_general(k_pad, v_pad, (((0,), (0,)), ((), ())), preferred_element_type=F32)
    s_out_ref[0, 0] = s_old * jnp.tile(gamma, (1, dv // LANES)) + upd


def _retention_sample(q, k, v, gate, gn_g, lg_tab, state, *, heads, dk, dv):
    b = state.shape[0]
    return pl.pallas_call(
        _ret_step_body,
        grid=(b, heads),
        in_specs=[pl.BlockSpec((1, 1, dk), lambda bi, h: (bi * heads + h, 0, 0)),
                  pl.BlockSpec((1, 1, dk), lambda bi, h: (bi * heads + h, 0, 0)),
                  pl.BlockSpec((1, 1, dv), lambda bi, h: (bi * heads + h, 0, 0)),
                  pl.BlockSpec((1, 1, dv), lambda bi, h: (bi * heads + h, 0, 0)),
                  pl.BlockSpec((1, dv), lambda bi, h: (0, h)),
                  pl.BlockSpec((1, 1, LANES), lambda bi, h: (h, 0, 0)),
                  pl.BlockSpec((1, 1, dk, dv), lambda bi, h: (bi, h, 0, 0))],
        out_specs=[pl.BlockSpec((1, 1, dv), lambda bi, h: (bi * heads + h, 0, 0)),
                   pl.BlockSpec((1, 1, dk, dv), lambda bi, h: (bi, h, 0, 0))],
        out_shape=[jax.ShapeDtypeStruct((b * heads, 1, dv), F32),
                   jax.ShapeDtypeStruct((b, heads, dk, dv), F32)],
        compiler_params=_params("parallel", "parallel"),
        name="retention_step",
    )(q, k, v, gate, gn_g, lg_tab, state)


def _compress_body(*refs, n_src, groups, dh):
    if n_src > 0:
        refs = refs[1:]
    n_src = max(n_src, 1)
    srcs = refs[:n_src]
    pos_ref, w1_ref, w2_ref, o_ref, carry_ref, rows_ref = refs[n_src:]
    j = pl.program_id(2)

    @pl.when(j == 0)
    def _():
        carry_ref[...] = jnp.zeros_like(carry_ref)

    rows_per_src = srcs[0].shape[-2]
    n = n_src * rows_per_src // CMP_STRIDE
    pos_lo = pos_ref[0, 0:1, :]
    pos_hi = pos_ref[0, 1:2, :]
    w1_lo = w1_ref[0, 0].astype(BF16)
    w1_hi = w1_ref[0, 1].astype(BF16)
    w2 = w2_ref[0].astype(BF16)
    for g in range(groups):
        for r, src in enumerate(srcs):
            rows_ref[r * rows_per_src:(r + 1) * rows_per_src, :] = src[0, :, g * dh:(g + 1) * dh]
        x = jnp.concatenate([rows_ref[pl.ds(i, n, stride=CMP_STRIDE), :] for i in range(CMP_STRIDE)],
                            axis=1)
        a = jnp.dot((x + pos_lo).astype(BF16), w1_lo, preferred_element_type=F32)
        bm = jnp.dot((x + pos_hi).astype(BF16), w1_hi, preferred_element_type=F32)
        prev = carry_ref[g]
        a_shift = jnp.concatenate([prev[0:1], a[:n - 1]], axis=0)
        carry_ref[g] = jnp.broadcast_to(a[n - 1:n], prev.shape)
        hid = jax.nn.gelu(a_shift + bm)
        o_ref[0, 0, g] = jnp.dot(hid.astype(BF16), w2, preferred_element_type=F32).astype(o_ref.dtype)


def _compress(src, page_table, cmp_pos, w_cmp1, w_cmp2, *, batch, groups, dh, rows_total, pages_per_step=16):
    hidden = w_cmp1.shape[-1]
    gw = groups * dh
    pos = cmp_pos.reshape(2, CMP_LEN // CMP_STRIDE, CMP_STRIDE * dh)
    w1 = w_cmp1.reshape(2, CMP_LEN // CMP_STRIDE, CMP_STRIDE * dh, hidden)
    n_sub_total = rows_total // CMP_STRIDE
    common = [pl.BlockSpec((1, 2, CMP_STRIDE * dh), lambda b, kv, j, *_: (kv, 0, 0)),
              pl.BlockSpec((1, 2, CMP_STRIDE * dh, hidden), lambda b, kv, j, *_: (kv, 0, 0, 0)),
              pl.BlockSpec((1, hidden, dh), lambda b, kv, j, *_: (kv, 0, 0))]
    out_shape = jax.ShapeDtypeStruct((batch, 2, groups, n_sub_total, dh), BF16)
    carry = pltpu.VMEM((groups, SUBLANES, hidden), F32)
    if page_table is None:
        scratch = [carry, pltpu.VMEM((rows_total, dh), F32)]
        n_steps = 1
        n_step_sub = n_sub_total
        src3 = src.reshape(batch, rows_total, 4 * gw)
        in_specs = [pl.BlockSpec((1, rows_total, gw), lambda b, kv, j: (b, 0, kv))] + common
        out_spec = pl.BlockSpec((1, 1, groups, n_step_sub, dh), lambda b, kv, j: (b, kv, 0, j, 0))
        return pl.pallas_call(
            functools.partial(_compress_body, n_src=0, groups=groups, dh=dh),
            grid=(batch, 2, n_steps),
            in_specs=in_specs, out_specs=out_spec, out_shape=out_shape, scratch_shapes=scratch,
            compiler_params=_params("parallel", "parallel", "arbitrary"),
            name="compress_prompt",
        )(src3, pos, w1, w_cmp2)
    page = src.shape[1]
    n_pages = page_table.shape[1]
    pps = _pick(n_pages, pages_per_step)
    n_steps = n_pages // pps
    n_step_sub = pps * page // CMP_STRIDE
    scratch = [carry, pltpu.VMEM((pps * page, dh), F32)]

    def page_spec(r):
        return pl.BlockSpec((1, page, gw), lambda b, kv, j, pt: (pt[b * n_pages + j * pps + r], 0, kv))

    in_specs = [page_spec(r) for r in range(pps)] + common
    out_spec = pl.BlockSpec((1, 1, groups, n_step_sub, dh), lambda b, kv, j, pt: (b, kv, 0, j, 0))
    return pl.pallas_call(
        functools.partial(_compress_body, n_src=pps, groups=groups, dh=dh),
        grid_spec=pltpu.PrefetchScalarGridSpec(
            num_scalar_prefetch=1, grid=(batch, 2, n_steps),
            in_specs=in_specs, out_specs=out_spec, scratch_shapes=scratch),
        out_shape=out_shape,
        compiler_params=_params("parallel", "parallel", "arbitrary"),
        name="compress_pages",
    )(page_table.reshape(-1), *([src] * pps), pos, w1, w_cmp2)


def _rel_bucket(dist):
    n = jnp.maximum(dist, 0)
    exact = REL_BUCKETS // 2
    nf = jnp.maximum(n, 1).astype(F32)
    large = exact + (jnp.log(nf / exact) / math.log(REL_MAX_DIST / exact) * (REL_BUCKETS - exact)).astype(jnp.int32)
    large = jnp.minimum(large, REL_BUCKETS - 1)
    return jnp.where(n < exact, n, large)


def _bias_tables_body(rb_ref, btab_ref, ec_ref):
    h = pl.program_id(0)
    t = KEY_TILE
    qi = lax.broadcasted_iota(jnp.int32, (t, t), 0)
    kj = lax.broadcasted_iota(jnp.int32, (t, t), 1)

    def bias_of(dist):
        bucket = _rel_bucket(dist)
        out = jnp.zeros(dist.shape, F32)
        for b in range(REL_BUCKETS):
            out = jnp.where(bucket == b, rb_ref[b, h], out)
        return out

    d0 = qi - kj
    far = bias_of(2 * t + d0)
    btab_ref[0, 0] = jnp.where(d0 >= 0, bias_of(d0), NEG_INF)
    btab_ref[0, 1] = bias_of(t + d0)
    btab_ref[0, 2] = far
    btab_ref[0, 3] = jnp.where(d0 <= 0, far, NEG_INF)
    ec_ref[0] = bias_of(qi - CMP_STRIDE * (kj - t // 2) - (CMP_LEN - 1 - CMP_STRIDE))


def _bias_tables(rel_bias):
    heads = rel_bias.shape[1]
    t = KEY_TILE
    assert 2 * t - (t - 1) >= REL_MAX_DIST
    return pl.pallas_call(
        _bias_tables_body,
        grid=(heads,),
        in_specs=[pl.BlockSpec(memory_space=pltpu.SMEM)],
        out_specs=[pl.BlockSpec((1, 4, t, t), lambda h: (h, 0, 0, 0)),
                   pl.BlockSpec((1, t, t), lambda h: (h, 0, 0))],
        out_shape=[jax.ShapeDtypeStruct((heads, 4, t, t), F32),
                   jax.ShapeDtypeStruct((heads, t, t), F32)],
        compiler_params=_params("parallel"),
        name="rel_bias_tables",
    )(rel_bias)


def _head_bias(dist, tbl_ref):
    bucket = _rel_bucket(dist)
    r = tbl_ref.shape[0]
    out = jnp.zeros((r, dist.shape[1]), F32)
    for b in range(REL_BUCKETS):
        out = jnp.where(bucket == b, tbl_ref[:, b:b + 1], out)
    return out


def _nsa_prompt_body(q_ref, gate_ref, kc_ref, vc_ref, ks_ref, vs_ref, kw_ref, vw_ref, btab_ref, ec_ref,
                     o_ref, selb_ref, m_ref, l_ref, acc_ref, *, rep, dh, n_top, nblk):
    tq = KEY_TILE
    ns = kc_ref.shape[3]
    qt = pl.program_id(2)
    qs = qt * tq
    q = jnp.concatenate([q_ref[0, :, r * dh:(r + 1) * dh] for r in range(rep)], axis=0)

    qi = lax.broadcasted_iota(jnp.int32, (tq, ns), 0)
    lane = lax.broadcasted_iota(jnp.int32, (tq, ns), 1)
    t_pos = qs + qi
    dist_c = t_pos - (CMP_STRIDE * lane + (CMP_LEN - 1 - CMP_STRIDE))
    valid_c = (dist_c >= 0) & (lane >= 1)
    shift = (_shr(qs, CMP_STRIDE) + tq // 2) & (ns - 1)
    far = dist_c >= REL_MAX_DIST
    s = lax.dot_general(q, kc_ref[0, 0, 0], (((1,), (1,)), ((), ())), preferred_element_type=F32)
    s3 = s.reshape(rep, tq, ns)
    bias_c = jnp.stack([jnp.where(far, btab_ref[r, 2], pltpu.roll(ec_ref[r], shift, 1)) for r in range(rep)])
    s3 = jnp.where(valid_c[None], s3 + bias_c, NEG_INF)
    m = jnp.max(s3, axis=-1, keepdims=True)
    e = jnp.where(valid_c[None], jnp.exp(s3 - m), 0.0)
    p = e / jnp.maximum(jnp.sum(e, axis=-1, keepdims=True), 1e-30)
    o_c = jnp.dot(p.reshape(rep * tq, ns).astype(BF16), vc_ref[0, 0, 0], preferred_element_type=F32)

    ratio = SEL_LEN // CMP_STRIDE
    psum = jnp.sum(p, axis=0)

    def ahead(kk):
        return jnp.where(lane + kk < ns, pltpu.roll(psum, ns - kk, 1), 0.0)

    imp = ahead(1) + 2.0 * (ahead(2) + ahead(3) + ahead(4)) + ahead(5)
    jb = _shr(lane, ratio)
    cur = _shr(t_pos, SEL_LEN)
    forced = (jb == 0) | (jb == cur) | (jb == cur - 1)
    elig = jb <= cur
    score = jnp.where(forced, 1e30, jnp.where(elig, imp, -1.0))
    rank = jnp.zeros((tq, ns), F32)
    for j2 in range(nblk):
        colv = score[:, ratio * j2:ratio * j2 + 1]
        ahead_of = (colv > score) | ((colv == score) & (ratio * j2 < lane))
        rank = rank + jnp.where(ahead_of, 1.0, 0.0)
    drop = jnp.where((rank < n_top) & elig, 0.0, NEG_INF)
    half_k = _shr(lax.broadcasted_iota(jnp.int32, (tq, KEY_TILE), 1), SEL_LEN)
    blocks_per_tile = KEY_TILE // SEL_LEN
    for kt in range(selb_ref.shape[0]):
        tile = None
        for u in reversed(range(blocks_per_tile)):
            lane0 = ratio * (blocks_per_tile * kt + u)
            colu = jnp.broadcast_to(drop[:, lane0:lane0 + 1], (tq, KEY_TILE))
            tile = colu if tile is None else jnp.where(half_k == u, colu, tile)
        selb_ref[kt] = tile

    def attend(k_ref, v_ref, lo, hi, tile_bias):
        m_ref[...] = jnp.full(m_ref.shape, NEG_INF, F32)
        l_ref[...] = jnp.zeros_like(l_ref)
        acc_ref[...] = jnp.zeros_like(acc_ref)

        def body(kt, carry):
            start = pl.multiple_of(kt * KEY_TILE, KEY_TILE)
            kk = k_ref[0, pl.ds(start, KEY_TILE), :].astype(BF16)
            vv = v_ref[0, pl.ds(start, KEY_TILE), :].astype(BF16)
            sc = lax.dot_general(q, kk, (((1,), (1,)), ((), ())), preferred_element_type=F32)
            sc = (sc.reshape(rep, tq, KEY_TILE) + tile_bias(kt)).reshape(rep * tq, KEY_TILE)
            m_old = m_ref[...]
            m_new = jnp.maximum(m_old, jnp.max(sc, axis=-1, keepdims=True))
            alpha = jnp.exp(m_old - m_new)
            pe = jnp.exp(sc - m_new)
            l_ref[...] = alpha * l_ref[...] + jnp.sum(pe, axis=-1, keepdims=True)
            acc_ref[...] = alpha * acc_ref[...] + jnp.dot(pe.astype(BF16), vv, preferred_element_type=F32)
            m_ref[...] = m_new
            return carry

        lax.fori_loop(lo, hi, body, 0)
        return acc_ref[...] * (1.0 / l_ref[...])

    def sel_bias(kt):
        return btab_ref[:, jnp.minimum(qt - kt, 2)] + selb_ref[kt][None]

    def win_bias(kt):
        delta = qt - kt
        return btab_ref[:, jnp.where(delta == WINDOW // KEY_TILE, 3, jnp.minimum(delta, 2))]

    o_s = attend(ks_ref, vs_ref, 0, qt + 1, sel_bias)
    o_w = attend(kw_ref, vw_ref, jnp.maximum(qt - WINDOW // KEY_TILE, 0), qt + 1, win_bias)

    for r in range(rep):
        rows = slice(r * tq, (r + 1) * tq)
        g_c = gate_ref[0, :, r:r + 1]
        g_s = gate_ref[0, :, rep + r:rep + r + 1]
        g_w = gate_ref[0, :, 2 * rep + r:2 * rep + r + 1]
        o_ref[0, :, r * dh:(r + 1) * dh] = (g_c * o_c[rows] + g_s * o_s[rows] + g_w * o_w[rows]).astype(o_ref.dtype)


def _nsa_prompt(q, gates, cmp_kv, rows4, win2, btab, ectab, *, batch, seq, groups, rep, dh):
    tq = KEY_TILE
    assert seq % tq == 0 and WINDOW % KEY_TILE == 0 and KEY_TILE % SEL_LEN == 0
    ns = seq // CMP_STRIDE
    assert ns % LANES == 0 and ns == LANES
    nblk = -(-seq // SEL_LEN)
    n_top = min(SEL_TOP, nblk)
    nq = seq // tq
    nkt = seq // KEY_TILE
    return pl.pallas_call(
        functools.partial(_nsa_prompt_body, rep=rep, dh=dh, n_top=n_top, nblk=nblk),
        grid=(batch, groups, nq),
        in_specs=[pl.BlockSpec((1, tq, rep * dh), lambda b, g, i: (b, i, g)),
                  pl.BlockSpec((1, tq, LANES), lambda b, g, i: (b, i, g)),
                  pl.BlockSpec((1, 1, 1, ns, dh), lambda b, g, i: (b, 0, g, 0, 0)),
                  pl.BlockSpec((1, 1, 1, ns, dh), lambda b, g, i: (b, 1, g, 0, 0)),
                  pl.BlockSpec((1, seq, dh), lambda b, g, i: (b, 0, 2 * groups + g)),
                  pl.BlockSpec((1, seq, dh), lambda b, g, i: (b, 0, 3 * groups + g)),
                  pl.BlockSpec((1, seq, dh), lambda b, g, i: (b, 0, g)),
                  pl.BlockSpec((1, seq, dh), lambda b, g, i: (b, 0, groups + g)),
                  pl.BlockSpec((rep, 4, tq, KEY_TILE), lambda b, g, i: (g, 0, 0, 0)),
                  pl.BlockSpec((rep, tq, KEY_TILE), lambda b, g, i: (g, 0, 0))],
        out_specs=pl.BlockSpec((1, tq, rep * dh), lambda b, g, i: (b, i, g)),
        out_shape=jax.ShapeDtypeStruct((batch, seq, groups * rep * dh), BF16),
        scratch_shapes=[pltpu.VMEM((nkt, tq, KEY_TILE), F32),
                        pltpu.VMEM((rep * tq, 1), F32),
                        pltpu.VMEM((rep * tq, 1), F32),
                        pltpu.VMEM((rep * tq, dh), F32)],
        compiler_params=_params("parallel", "parallel", "arbitrary"),
        name="nsa_prompt",
    )(q, gates, cmp_kv, cmp_kv, rows4, rows4, win2, win2, btab, ectab)


def _softmax_rows(s, valid):
    s = jnp.where(valid, s, NEG_INF)
    m = jnp.max(s, axis=-1, keepdims=True)
    e = jnp.where(valid, jnp.exp(s - m), 0.0)
    return e / jnp.maximum(jnp.sum(e, axis=-1, keepdims=True), 1e-30)


def _nsa_sample_cmp_body(q_ref, kc_ref, vc_ref, tbl_ref, oc_ref, ids_ref, *, t_pos, n_sel):
    ns = kc_ref.shape[3]
    rep = q_ref.shape[1]
    q = q_ref[0].astype(BF16)
    lane = lax.broadcasted_iota(jnp.int32, (1, ns), 1)
    dist_c = t_pos - (CMP_STRIDE * lane + (CMP_LEN - 1 - CMP_STRIDE))
    valid = (dist_c >= 0) & (lane >= 1)
    s = lax.dot_general(q, kc_ref[0, 0, 0], (((1,), (1,)), ((), ())), preferred_element_type=F32)
    p = _softmax_rows(s + _head_bias(dist_c, tbl_ref), valid)
    oc_ref[0, 0] = jnp.dot(p.astype(BF16), vc_ref[0, 0, 0], preferred_element_type=F32)

    ratio = SEL_LEN // CMP_STRIDE
    psum = jnp.broadcast_to(jnp.sum(p, axis=0, keepdims=True), (rep, ns))
    lane_r = lax.broadcasted_iota(jnp.int32, (rep, ns), 1)

    def ahead(kk):
        return jnp.where(lane_r + kk < ns, pltpu.roll(psum, ns - kk, 1), 0.0)

    imp = ahead(1) + 2.0 * (ahead(2) + ahead(3) + ahead(4)) + ahead(5)
    jb = _shr(lane_r, ratio)
    cur = t_pos // SEL_LEN
    forced = (jb == 0) | (jb == cur - 1)
    cand = ((lane_r & (ratio - 1)) == 0) & (jb < cur)
    score = jnp.where(cand, jnp.where(forced, 1e30, imp), -1.0)
    lane_f = lane_r.astype(F32)
    lane_o = lax.broadcasted_iota(jnp.int32, (rep, LANES), 1)
    ids = jnp.full((rep, LANES), cur, jnp.int32)
    for n in range(n_sel):
        best = jnp.max(score, axis=-1, keepdims=True)
        first = jnp.min(jnp.where(score == best, lane_f, float(ns)), axis=-1, keepdims=True)
        ids = jnp.where(lane_o == n, _shr(first.astype(jnp.int32), ratio), ids)
        score = jnp.where(lane_f == first, -2.0, score)
    ids_ref[0, 0] = ids


def _nsa_sample_sel_body(pt_ref, ids_ref, q_ref, k_ref, v_ref, new_ref, kw_ref, vw_ref, oc_ref, gate_ref, tbl_ref,
                         o_ref, m_ref, l_ref, acc_ref, *, t_pos, groups, n_sel, w_start):
    b = pl.program_id(0)
    g = pl.program_id(1)
    n = pl.program_id(2)
    rep, dh = q_ref.shape[1], q_ref.shape[2]
    q = q_ref[0].astype(BF16)
    cur = t_pos // SEL_LEN

    def new_row(slot):
        return new_ref[0, pl.ds(slot * groups + g, 1), :].astype(BF16)

    def self_score(k_new):
        s0 = jnp.sum(q.astype(F32) * k_new.astype(F32), axis=-1, keepdims=True)
        return s0 + _head_bias(jnp.zeros((1, 1), jnp.int32), tbl_ref)

    @pl.when(n == 0)
    def _():
        m_ref[...] = self_score(new_row(2))
        l_ref[...] = jnp.ones_like(l_ref)
        acc_ref[...] = jnp.broadcast_to(new_row(3).astype(F32), acc_ref.shape)

    blk = ids_ref[(b * groups + g) * LANES + n]
    kk = k_ref[0, 0].astype(BF16)
    vv = v_ref[0, 0].astype(BF16)
    lane = lax.broadcasted_iota(jnp.int32, (1, SEL_LEN), 1)
    dist = jnp.where(blk < cur, t_pos - blk * SEL_LEN, -1) - lane
    valid = dist >= 0
    sc = lax.dot_general(q, kk, (((1,), (1,)), ((), ())), preferred_element_type=F32) + _head_bias(dist, tbl_ref)
    sc = jnp.where(valid, sc, NEG_INF)
    m_old = m_ref[...]
    m_new = jnp.maximum(m_old, jnp.max(sc, axis=-1, keepdims=True))
    alpha = jnp.exp(m_old - m_new)
    pe = jnp.where(valid, jnp.exp(sc - m_new), 0.0)
    l_new = alpha * l_ref[...] + jnp.sum(pe, axis=-1, keepdims=True)
    acc_new = alpha * acc_ref[...] + jnp.dot(pe.astype(BF16), vv, preferred_element_type=F32)
    m_ref[...] = m_new
    l_ref[...] = l_new
    acc_ref[...] = acc_new

    @pl.when(n == n_sel - 1)
    def _():
        o_s = acc_new / l_new
        nw = kw_ref.shape[1]
        lane_w = lax.broadcasted_iota(jnp.int32, (1, nw), 1)
        dist_w = t_pos - (w_start + lane_w)
        valid_w = (dist_w >= 0) & (dist_w <= WINDOW)
        sw = lax.dot_general(q, kw_ref[0].astype(BF16), (((1,), (1,)), ((), ())), preferred_element_type=F32)
        sw = jnp.where(valid_w, sw + _head_bias(dist_w, tbl_ref), NEG_INF)
        s_self = self_score(new_row(4))
        mw = jnp.maximum(jnp.max(sw, axis=-1, keepdims=True), s_self)
        ew = jnp.where(valid_w, jnp.exp(sw - mw), 0.0)
        e_self = jnp.exp(s_self - mw)
        lw = jnp.sum(ew, axis=-1, keepdims=True) + e_self
        pw = ew / lw
        p_self = e_self / lw
        o_w = (jnp.dot(pw.astype(BF16), vw_ref[0].astype(BF16), preferred_element_type=F32)
               + p_self.astype(BF16).astype(F32) * new_row(5).astype(F32))
        gates = gate_ref[0, 0]
        o_ref[0, 0] = (gates[:, 0:1] * oc_ref[0, 0] + gates[:, 1:2] * o_s + gates[:, 2:3] * o_w).astype(o_ref.dtype)


def _nsa_sample(q, gates, cmp_kv, kv_new, cache_kv, cache_win, page_table, rel_bias, *, groups, rep, dh, past_len):
    batch = q.shape[0]
    n_pool, page = cache_kv.shape[0], cache_kv.shape[1]
    n_pages = page_table.shape[1]
    t_pos = past_len
    assert past_len % SEL_LEN == 0 and page % SEL_LEN == 0
    ns = past_len // CMP_STRIDE
    assert ns % LANES == 0
    nblk = past_len // SEL_LEN + 1
    n_sel = min(SEL_TOP, nblk) - 1
    assert 1 <= n_sel <= LANES
    tbl = rel_bias.T
    o_c, ids = pl.pallas_call(
        functools.partial(_nsa_sample_cmp_body, t_pos=t_pos, n_sel=n_sel),
        grid=(batch, groups),
        in_specs=[pl.BlockSpec((1, rep, dh), lambda b, g: (b, g, 0)),
                  pl.BlockSpec((1, 1, 1, ns, dh), lambda b, g: (b, 0, g, 0, 0)),
                  pl.BlockSpec((1, 1, 1, ns, dh), lambda b, g: (b, 1, g, 0, 0)),
                  pl.BlockSpec((rep, REL_BUCKETS), lambda b, g: (g, 0))],
        out_specs=[pl.BlockSpec((1, 1, rep, dh), lambda b, g: (b, g, 0, 0)),
                   pl.BlockSpec((1, 1, rep, LANES), lambda b, g: (b, g, 0, 0))],
        out_shape=[jax.ShapeDtypeStruct((batch, groups, rep, dh), F32),
                   jax.ShapeDtypeStruct((batch, groups, rep, LANES), jnp.int32)],
        compiler_params=_params("parallel", "parallel"),
        name="nsa_sample_compressed",
    )(q, cmp_kv, cmp_kv, tbl)

    halves = page // SEL_LEN
    pool = cache_kv.reshape(n_pool, halves, SEL_LEN, 4 * groups * dh)
    win = cache_win.reshape(batch, cache_win.shape[1], 2 * groups * dh)
    w_start = past_len - cache_win.shape[1]
    ids_flat = ids[:, :, 0, :].reshape(-1)

    def blk_map(slot):
        def index_map(b, g, n, pt, sel):
            blk = jnp.minimum(sel[(b * groups + g) * LANES + n], nblk - 2)
            return (pt[b * n_pages + blk // halves], blk % halves, 0, slot * groups + g)
        return index_map

    out = pl.pallas_call(
        functools.partial(_nsa_sample_sel_body, t_pos=t_pos, groups=groups, n_sel=n_sel, w_start=w_start),
        grid_spec=pltpu.PrefetchScalarGridSpec(
            num_scalar_prefetch=2, grid=(batch, groups, n_sel),
            in_specs=[pl.BlockSpec((1, rep, dh), lambda b, g, n, pt, sel: (b, g, 0)),
                      pl.BlockSpec((1, 1, SEL_LEN, dh), blk_map(2)),
                      pl.BlockSpec((1, 1, SEL_LEN, dh), blk_map(3)),
                      pl.BlockSpec((1, 6 * groups, dh), lambda b, g, n, pt, sel: (b, 0, 0)),
                      pl.BlockSpec((1, win.shape[1], dh), lambda b, g, n, pt, sel: (b, 0, g)),
                      pl.BlockSpec((1, win.shape[1], dh), lambda b, g, n, pt, sel: (b, 0, groups + g)),
                      pl.BlockSpec((1, 1, rep, dh), lambda b, g, n, pt, sel: (b, g, 0, 0)),
                      pl.BlockSpec((1, 1, rep, 3), lambda b, g, n, pt, sel: (b, g, 0, 0)),
                      pl.BlockSpec((rep, REL_BUCKETS), lambda b, g, n, pt, sel: (g, 0))],
            out_specs=pl.BlockSpec((1, 1, rep, dh), lambda b, g, n, pt, sel: (b, g, 0, 0)),
            scratch_shapes=[pltpu.VMEM((rep, 1), F32), pltpu.VMEM((rep, 1), F32), pltpu.VMEM((rep, dh), F32)]),
        out_shape=jax.ShapeDtypeStruct((batch, groups, rep, dh), F32),
        compiler_params=_params("parallel", "parallel", "arbitrary"),
        name="nsa_sample_selected_window",
    )(page_table.reshape(-1), ids_flat, q, pool, pool, kv_new, win, win, o_c, gates, tbl)
    return out


def _rotary_tables(pos, dk):
    half = dk // 2
    inv = ROPE_BASE ** (-jnp.arange(half, dtype=F32) / half)
    ang = pos.astype(F32)[:, None] * inv[None, :]
    return jnp.cos(ang), jnp.sin(ang)


def _mlp_block(x32, x16, w1, w2, g, b, alpha, *, want_bf16, tag):
    hid = _matmul(x16, w1, w1.shape[1], out_dtype=BF16, epilogue=_ep_relu2, name=tag + "_ff1")
    y = _matmul(hid, w2, w2.shape[1], name=tag + "_ff2")
    return _add_ln(x32, y, g, b, alpha, want_bf16=want_bf16, name=tag + "_ln_ff")


def kernel(x_prompt, x_sample, state_ret, cache_kv, cache_win, page_table, w_ret_in, ret_gn_g, w_ret_out, w_kv,
           cmp_pos, w_cmp1, w_cmp2, w_qg, w_nsa_out, rel_bias, w_ff1, w_ff2, ln_g, ln_b):
    bsz, seq, d = x_prompt.shape
    dbsz, dseq, _ = x_sample.shape
    assert dseq == 1
    n_a, _, ret_heads, dk, dv = state_ret.shape
    depth = ln_g.shape[0]
    assert n_a == 1 and depth == 2
    groups, dh = cache_kv.shape[3], cache_kv.shape[4]
    heads = rel_bias.shape[1]
    rep = heads // groups
    past_len = page_table.shape[1] * cache_kv.shape[1]
    alpha = (2.0 * depth) ** 0.25
    hkv = groups * dh
    assert rep * 3 <= LANES and ret_heads * dk == d and heads * dh == d

    lg = jnp.log1p(-jnp.exp2(-5.0 - jnp.arange(ret_heads, dtype=F32)))
    lg_tab = jnp.broadcast_to(lg[:, None, None], (ret_heads, 1, LANES))
    gn_g = ret_gn_g[0].reshape(1, ret_heads * dv)
    w_in, w_out = w_ret_in[0], w_ret_out[0]
    w_q, w_o = w_qg[0], w_nsa_out[0]
    w_gate_raw = w_q[:, heads * dh:]
    w_gate_p = jnp.pad(w_gate_raw.reshape(d, groups, rep, 3).transpose(0, 1, 3, 2).reshape(d, groups, 3 * rep),
                       ((0, 0), (0, 0), (0, LANES - 3 * rep))).reshape(d, groups * LANES)
    w_gate_s = jnp.pad(w_gate_raw, ((0, 0), (0, (-3 * heads) % LANES)))

    def trunk(x, pos0, is_prompt):
        b, t, _ = x.shape
        m = b * t
        x32 = x.reshape(m, d)
        x16 = x32.astype(BF16)
        pos = pos0 + jnp.arange(t)
        cos, sin = _rotary_tables(pos, dk)
        tag = "p" if is_prompt else "s"
        tm = _pick(m, 2048)
        rows_per_seq_blocks = max(t // tm, 1)
        if t >= tm:
            rot_specs = [pl.BlockSpec((tm, dk // 2), lambda i, j, k: (i % rows_per_seq_blocks, 0))] * 2
            rot_tabs = (cos, sin)
        else:
            rot_tabs = (jnp.tile(cos, (b, 1)), jnp.tile(sin, (b, 1)))
            rot_specs = [pl.BlockSpec((tm, dk // 2), lambda i, j, k: (i, 0))] * 2
        act = BF16 if is_prompt else F32
        qk = _matmul(x16, w_in, 2 * d, out_dtype=act, extra=rot_tabs, extra_specs=rot_specs,
                     epilogue=functools.partial(_ep_rotary, head_dim=dk, k_col_start=d, k_scale=dk ** -0.5),
                     name=tag + "_ret_qk")
        v = _matmul(x16, w_in, 2 * d, col_off=2 * d, out_dtype=act, name=tag + "_ret_v")
        gate = _matmul(x16, w_in, 2 * d, col_off=4 * d, name=tag + "_ret_gate")
        if is_prompt:
            o, st = _retention_prompt(qk.reshape(b, t, 2 * d), v.reshape(b, t, 2 * d), gate.reshape(b, t, 2 * d),
                                      gn_g, lg_tab, heads=ret_heads, dk=dk, dv=dv)
            o = o.reshape(m, 2 * d)
        else:
            q = qk[:, :d].reshape(b * ret_heads, 1, dk)
            k = qk[:, d:].reshape(b * ret_heads, 1, dk)
            o, st = _retention_sample(q, k, v.reshape(b * ret_heads, 1, dv), gate.reshape(b * ret_heads, 1, dv),
                                      gn_g, lg_tab, state_ret[0], heads=ret_heads, dk=dk, dv=dv)
            o = o.reshape(m, 2 * d)
        a = _matmul(o, w_out, d, name=tag + "_ret_out")
        x32, x16 = _add_ln(x32, a, ln_g[0, 0], ln_b[0, 0], alpha, name=tag + "_ln_ret")
        x32, x16 = _mlp_block(x32, x16, w_ff1[0], w_ff2[0], ln_g[0, 1], ln_b[0, 1], alpha, want_bf16=True, tag=tag + "0")
        rows4 = _matmul(x16, w_kv, 4 * hkv, name=tag + "_kv_rows")
        win2 = _matmul(x16, w_kv, 2 * hkv, col_off=4 * hkv, name=tag + "_kv_win")
        q = _matmul(x16, w_q, heads * dh, out_dtype=act, epilogue=functools.partial(_ep_scale, scale=dh ** -0.5),
                    name=tag + "_nsa_q")
        if is_prompt:
            gates = _matmul(x16, w_gate_p, groups * LANES, epilogue=_ep_sigmoid, name=tag + "_nsa_gates")
            cmp_kv = _compress(rows4, None, cmp_pos, w_cmp1, w_cmp2, batch=b, groups=groups, dh=dh, rows_total=t)
            btab, ectab = _bias_tables(rel_bias)
            o = _nsa_prompt(q.reshape(b, t, heads * dh), gates.reshape(b, t, groups * LANES), cmp_kv,
                            rows4.reshape(b, t, 4 * hkv), win2.reshape(b, t, 2 * hkv), btab, ectab,
                            batch=b, seq=t, groups=groups, rep=rep, dh=dh)
            o = o.reshape(m, heads * dh)
            wp = min(WINDOW, t)
            new_rows = (rows4.reshape(b, t, 4, groups, dh), win2.reshape(b, t, 2, groups, dh)[:, t - wp:])
        else:
            gates = _matmul(x16, w_gate_s, w_gate_s.shape[1], epilogue=_ep_sigmoid, name=tag + "_nsa_gates")
            gates = gates[:, :3 * heads].reshape(b, groups, rep, 3)
            pool = cache_kv.reshape(cache_kv.shape[0], cache_kv.shape[1], 4 * hkv)
            cmp_kv = _compress(pool, page_table, cmp_pos, w_cmp1, w_cmp2, batch=b, groups=groups, dh=dh,
                               rows_total=past_len)
            kv_new = jnp.concatenate([rows4, win2], axis=1).reshape(b, 6 * groups, dh)
            o = _nsa_sample(q.reshape(b, heads, dh), gates, cmp_kv, kv_new, cache_kv, cache_win, page_table, rel_bias,
                            groups=groups, rep=rep, dh=dh, past_len=past_len)
            o = o.reshape(m, heads * dh)
            new_rows = (rows4.reshape(b, t, 4, groups, dh), win2.reshape(b, t, 2, groups, dh))
        a = _matmul(o, w_o, d, name=tag + "_nsa_out")
        x32, x16 = _add_ln(x32, a, ln_g[1, 0], ln_b[1, 0], alpha, name=tag + "_ln_nsa")
        y, _ = _mlp_block(x32, x16, w_ff1[1], w_ff2[1], ln_g[1, 1], ln_b[1, 1], alpha, want_bf16=False, tag=tag + "1")
        return y.reshape(b, t, d), st[None], new_rows

    y_p, st_p, (kv_p, win_p) = trunk(x_prompt, 0, True)
    y_s, st_s, (kv_s, win_s) = trunk(x_sample, past_len, False)
    return (y_p, y_s, st_p, kv_p, win_p, st_s, kv_s, win_s)
```

```python
import functools
import math

import jax
import jax.numpy as jnp
import numpy as np
from jax import lax
from jax.experimental import pallas as pl
from jax.experimental.pallas import tpu as pltpu

RET_CHUNK = 128
ROPE_BASE = 10000.0
CMP_STRIDE = 16
CMP_LEN = 32
SEL_LEN = 64
SEL_TOP = 16
WINDOW = 512
REL_BUCKETS = 32
REL_MAX_DIST = 128
LN_EPS = 1e-5
GN_EPS = 1e-6
NEG_INF = -1e30

LANES = 128
SUBLANES = 8
VMEM_LIMIT_BYTES = 56 * 1024 * 1024

F32 = jnp.float32
BF16 = jnp.bfloat16
KEY_TILE = LANES


def _shr(x, pow2):
    assert pow2 & (pow2 - 1) == 0
    return x >> (pow2.bit_length() - 1)


def _params(*sem):
    return pltpu.CompilerParams(dimension_semantics=sem, vmem_limit_bytes=VMEM_LIMIT_BYTES)


def _ep_store(acc, o_ref, j):
    o_ref[...] = acc.astype(o_ref.dtype)


def _ep_relu2(acc, o_ref, j):
    r = jnp.maximum(acc, 0.0)
    o_ref[...] = (r * r).astype(o_ref.dtype)


def _ep_scale(acc, o_ref, j, *, scale):
    o_ref[...] = (acc * scale).astype(o_ref.dtype)


def _ep_sigmoid(acc, o_ref, j):
    o_ref[...] = jax.nn.sigmoid(acc).astype(o_ref.dtype)


def _ep_rotary(acc, o_ref, j, cos_ref, sin_ref, *, head_dim, k_col_start, k_scale):
    tn = acc.shape[1]
    half = head_dim // 2
    cos = cos_ref[...]
    sin = sin_ref[...]
    scale = jnp.where(j * tn >= k_col_start, k_scale, 1.0).astype(F32)
    for c in range(tn // head_dim):
        a1 = acc[:, c * head_dim:c * head_dim + half]
        a2 = acc[:, c * head_dim + half:(c + 1) * head_dim]
        o_ref[:, c * head_dim:c * head_dim + half] = ((a1 * cos - a2 * sin) * scale).astype(o_ref.dtype)
        o_ref[:, c * head_dim + half:(c + 1) * head_dim] = ((a1 * sin + a2 * cos) * scale).astype(o_ref.dtype)


def _mm_body(*refs, nk, n_extra, epilogue):
    x_ref, w_ref = refs[0], refs[1]
    extra = refs[2:2 + n_extra]
    o_ref = refs[2 + n_extra]
    j = pl.program_id(1)
    part = jnp.dot(x_ref[...].astype(BF16), w_ref[...].astype(BF16), preferred_element_type=F32)
    if nk == 1:
        epilogue(part, o_ref, j, *extra)
        return
    acc_ref = refs[3 + n_extra]
    k = pl.program_id(2)

    @pl.when(k == 0)
    def _():
        acc_ref[...] = part

    @pl.when((k > 0) & (k < nk - 1))
    def _():
        acc_ref[...] += part

    @pl.when(k == nk - 1)
    def _():
        epilogue(acc_ref[...] + part, o_ref, j, *extra)


def _pick(n, pref):
    t = min(n, pref)
    while n % t:
        t //= 2
    return t


def _matmul(x, w, n_out, *, layer=0, col_off=0, out_dtype=F32, epilogue=_ep_store, extra=(), extra_specs=(),
            tm=2048, tn=1024, tk=1024, name="matmul"):
    if w.ndim == 2:
        w = w[None]
    m, kdim = x.shape
    tm = _pick(m, tm)
    tn = _pick(n_out, tn)
    tk = _pick(kdim, tk)
    assert col_off % tn == 0
    nk = kdim // tk
    joff = col_off // tn
    scratch = [pltpu.VMEM((tm, tn), F32)] if nk > 1 else []
    return pl.pallas_call(
        functools.partial(_mm_body, nk=nk, n_extra=len(extra), epilogue=epilogue),
        grid=(m // tm, n_out // tn, nk),
        in_specs=[pl.BlockSpec((tm, tk), lambda i, j, k: (i, k)),
                  pl.BlockSpec((None, tk, tn), lambda i, j, k: (layer, k, j + joff))] + list(extra_specs),
        out_specs=pl.BlockSpec((tm, tn), lambda i, j, k: (i, j)),
        out_shape=jax.ShapeDtypeStruct((m, n_out), out_dtype),
        scratch_shapes=scratch,
        compiler_params=_params("parallel", "parallel", "arbitrary"),
        name=name,
    )(x, w, *extra)


def _ln_body(x_ref, a_ref, g_ref, b_ref, *o_refs, alpha):
    z = alpha * x_ref[...] + a_ref[...]
    mu = jnp.mean(z, axis=-1, keepdims=True)
    zc = z - mu
    var = jnp.mean(zc * zc, axis=-1, keepdims=True)
    y = zc * lax.rsqrt(var + LN_EPS) * g_ref[...] + b_ref[...]
    for o_ref in o_refs:
        o_ref[...] = y.astype(o_ref.dtype)


def _add_ln(x, a, g, b, alpha, *, want_bf16=True, name="add_ln"):
    m, d = x.shape
    tm = _pick(m, 256)
    row = pl.BlockSpec((tm, d), lambda i: (i, 0))
    vec = pl.BlockSpec((1, d), lambda i: (0, 0))
    shapes = [jax.ShapeDtypeStruct((m, d), F32)]
    if want_bf16:
        shapes.append(jax.ShapeDtypeStruct((m, d), BF16))
    outs = pl.pallas_call(
        functools.partial(_ln_body, alpha=alpha),
        grid=(m // tm,),
        in_specs=[row, row, vec, vec],
        out_specs=[row] * len(shapes),
        out_shape=shapes,
        compiler_params=_params("parallel"),
        name=name,
    )(x, a, g.reshape(1, d), b.reshape(1, d))
    return outs if want_bf16 else (outs[0], None)


def _group_norm_gate(o, gate, gn_g):
    mu = jnp.mean(o, axis=-1, keepdims=True)
    oc = o - mu
    var = jnp.mean(oc * oc, axis=-1, keepdims=True)
    return jax.nn.silu(gate) * (oc * lax.rsqrt(var + GN_EPS) * gn_g)


def _ret_chunk_body(q_ref, k_ref, v_ref, g_ref, gn_ref, lg_ref, o_ref, s_out_ref, s_ref, *, nc, chunk):
    c = pl.program_id(2)
    dk = q_ref.shape[2]
    dv = v_ref.shape[2]

    @pl.when(c == 0)
    def _():
        s_ref[...] = jnp.zeros_like(s_ref)

    lg = lg_ref[0]
    q = q_ref[0]
    k = k_ref[0]
    v = v_ref[0]
    row = lax.broadcasted_iota(jnp.int32, (chunk, chunk), 0)
    col = lax.broadcasted_iota(jnp.int32, (chunk, chunk), 1)
    diff = (row - col).astype(F32)
    dmask = jnp.where(diff >= 0, jnp.exp(jnp.maximum(diff, 0.0) * lg), 0.0)
    ridx = lax.broadcasted_iota(jnp.int32, (chunk, LANES), 0).astype(F32)
    xi = jnp.exp((ridx + 1.0) * lg)
    zeta = jnp.exp((chunk - 1.0 - ridx) * lg)
    g_c = jnp.exp(chunk * lg)

    att = lax.dot_general(q, k, (((1,), (1,)), ((), ())), preferred_element_type=F32) * dmask
    s_old = s_ref[...]
    cross = jnp.dot(q, s_old.astype(BF16), preferred_element_type=F32)
    o = jnp.dot(att.astype(BF16), v, preferred_element_type=F32) + cross * jnp.tile(xi, (1, dv // LANES))
    kz = (k.astype(F32) * jnp.tile(zeta, (1, dk // LANES))).astype(BF16)
    upd = lax.dot_general(kz, v, (((0,), (0,)), ((), ())), preferred_element_type=F32)
    s_new = s_old * jnp.tile(g_c, (1, dv // LANES)) + upd
    s_ref[...] = s_new
    o_ref[0] = _group_norm_gate(o, g_ref[0], gn_ref[...]).astype(o_ref.dtype)

    @pl.when(c == nc - 1)
    def _():
        s_out_ref[0, 0] = s_new


def _retention_prompt(qk, v, gate, gn_g, lg_tab, *, heads, dk, dv):
    b, t, _ = qk.shape
    chunk = min(RET_CHUNK, t)
    nc = t // chunk
    return pl.pallas_call(
        functools.partial(_ret_chunk_body, nc=nc, chunk=chunk),
        grid=(b, heads, nc),
        in_specs=[pl.BlockSpec((1, chunk, dk), lambda bi, h, c: (bi, c, h)),
                  pl.BlockSpec((1, chunk, dk), lambda bi, h, c: (bi, c, heads + h)),
                  pl.BlockSpec((1, chunk, dv), lambda bi, h, c: (bi, c, h)),
                  pl.BlockSpec((1, chunk, dv), lambda bi, h, c: (bi, c, h)),
                  pl.BlockSpec((1, dv), lambda bi, h, c: (0, h)),
                  pl.BlockSpec((1, 1, LANES), lambda bi, h, c: (h, 0, 0))],
        out_specs=[pl.BlockSpec((1, chunk, dv), lambda bi, h, c: (bi, c, h)),
                   pl.BlockSpec((1, 1, dk, dv), lambda bi, h, c: (bi, h, 0, 0))],
        out_shape=[jax.ShapeDtypeStruct((b, t, heads * dv), BF16),
                   jax.ShapeDtypeStruct((b, heads, dk, dv), F32)],
        scratch_shapes=[pltpu.VMEM((dk, dv), F32)],
        compiler_params=_params("parallel", "parallel", "arbitrary"),
        name="retention_chunks",
    )(qk, qk, v, gate, gn_g, lg_tab)


def _ret_step_body(q_ref, k_ref, v_ref, g_ref, gn_ref, lg_ref, s_ref, o_ref, s_out_ref):
    dk = q_ref.shape[2]
    dv = v_ref.shape[2]
    lg = lg_ref[0]
    gamma = jnp.exp(lg)
    q = q_ref[0].astype(BF16)
    k = k_ref[0].astype(BF16)
    v = v_ref[0].astype(BF16)
    s_old = s_ref[0, 0]
    att = jnp.sum(q.astype(F32) * k.astype(F32), axis=-1, keepdims=True)
    q8 = jnp.broadcast_to(q, (SUBLANES * 2, dk))
    cross = jnp.dot(q8, s_old.astype(BF16), preferred_element_type=F32)[0:1]
    o = att.astype(BF16).astype(F32) * v.astype(F32) + cross * jnp.tile(gamma, (1, dv // LANES))
    o_ref[0] = _group_norm_gate(o, g_ref[0], gn_ref[...]).astype(o_ref.dtype)
    rows = lax.broadcasted_iota(jnp.int32, (LANES, dk), 0)
    k_pad = jnp.where(rows == 0, jnp.broadcast_to(k.astype(F32), (LANES, dk)), 0.0).astype(BF16)
    v_pad = jnp.broadcast_to(v, (LANES, dv))
    upd = lax.dot_general(k_pad, v_pad, (((0,), (0,)), ((), ())), preferred_element_type=F32)
    s_out_ref[0, 0] = s_old * jnp.tile(gamma, (1, dv // LANES)) + upd


def _retention_sample(q, k, v, gate, gn_g, lg_tab, state, *, heads, dk, dv):
    b = state.shape[0]
    return pl.pallas_call(
        _ret_step_body,
        grid=(b, heads),
        in_specs=[pl.BlockSpec((1, 1, dk), lambda bi, h: (bi * heads + h, 0, 0)),
                  pl.BlockSpec((1, 1, dk), lambda bi, h: (bi * heads + h, 0, 0)),
                  pl.BlockSpec((1, 1, dv), lambda bi, h: (bi * heads + h, 0, 0)),
                  pl.BlockSpec((1, 1, dv), lambda bi, h: (bi * heads + h, 0, 0)),
                  pl.BlockSpec((1, dv), lambda bi, h: (0, h)),
                  pl.BlockSpec((1, 1, LANES), lambda bi, h: (h, 0, 0)),
                  pl.BlockSpec((1, 1, dk, dv), lambda bi, h: (bi, h, 0, 0))],
        out_specs=[pl.BlockSpec((1, 1, dv), lambda bi, h: (bi * heads + h, 0, 0)),
                   pl.BlockSpec((1, 1, dk, dv), lambda bi, h: (bi, h, 0, 0))],
        out_shape=[jax.ShapeDtypeStruct((b * heads, 1, dv), F32),
                   jax.ShapeDtypeStruct((b, heads, dk, dv), F32)],
        compiler_params=_params("parallel", "parallel"),
        name="retention_step",
    )(q, k, v, gate, gn_g, lg_tab, state)


def _compress_prompt_body(src_ref, pos_ref, w1_ref, w2_ref, o_ref, rows_ref, *, groups, dh):
    n = src_ref.shape[1] // CMP_STRIDE
    pos_lo = pos_ref[0, 0:1, :]
    pos_hi = pos_ref[0, 1:2, :]
    w1_lo = w1_ref[0, 0].astype(BF16)
    w1_hi = w1_ref[0, 1].astype(BF16)
    w2 = w2_ref[0].astype(BF16)
    for g in range(groups):
        rows_ref[...] = src_ref[0, :, g * dh:(g + 1) * dh]
        x = jnp.concatenate([rows_ref[pl.ds(i, n, stride=CMP_STRIDE), :] for i in range(CMP_STRIDE)],
                            axis=1)
        a = jnp.dot((x + pos_lo).astype(BF16), w1_lo, preferred_element_type=F32)
        bm = jnp.dot((x + pos_hi).astype(BF16), w1_hi, preferred_element_type=F32)
        a_shift = jnp.concatenate([jnp.zeros_like(a[0:1]), a[:n - 1]], axis=0)
        hid = jax.nn.gelu(a_shift + bm)
        o_ref[0, 0, g] = jnp.dot(hid.astype(BF16), w2, preferred_element_type=F32).astype(o_ref.dtype)


def _compress_prompt(rows, cmp_pos, w_cmp1, w_cmp2, *, batch, groups, dh, seq):
    hidden = w_cmp1.shape[-1]
    gw = groups * dh
    halves = CMP_LEN // CMP_STRIDE
    pos = cmp_pos.reshape(2, halves, CMP_STRIDE * dh)
    w1 = w_cmp1.reshape(2, halves, CMP_STRIDE * dh, hidden)
    n_sub = seq // CMP_STRIDE
    return pl.pallas_call(
        functools.partial(_compress_prompt_body, groups=groups, dh=dh),
        grid=(batch, 2),
        in_specs=[pl.BlockSpec((1, seq, gw), lambda b, kv: (b, 0, kv)),
                  pl.BlockSpec((1, halves, CMP_STRIDE * dh), lambda b, kv: (kv, 0, 0)),
                  pl.BlockSpec((1, halves, CMP_STRIDE * dh, hidden), lambda b, kv: (kv, 0, 0, 0)),
                  pl.BlockSpec((1, hidden, dh), lambda b, kv: (kv, 0, 0))],
        out_specs=pl.BlockSpec((1, 1, groups, n_sub, dh), lambda b, kv: (b, kv, 0, 0, 0)),
        out_shape=jax.ShapeDtypeStruct((batch, 2, groups, n_sub, dh), BF16),
        scratch_shapes=[pltpu.VMEM((seq, dh), F32)],
        compiler_params=_params("parallel", "parallel"),
        name="compress_prompt",
    )(rows, pos, w1, w_cmp2)


def _compress_pages_body(pt_ref, *refs, pps, groups, dh):
    del pt_ref
    srcs = refs[:pps]
    pos_ref, w1_ref, w2_ref, o_ref, carry_ref, lhs_ref, h_ref = refs[pps:]
    j = pl.program_id(1)

    @pl.when(j == 0)
    def _():
        carry_ref[...] = jnp.zeros_like(carry_ref)

    nq = 2 * groups
    subs = srcs[0].shape[1]
    n = pps * subs
    m = n * nq
    hidden = w1_ref.shape[-1]
    is_k = (lax.broadcasted_iota(jnp.int32, (m, dh), 0) & (nq - 1)) < groups
    for half in range(CMP_LEN // CMP_STRIDE):
        for i in range(CMP_STRIDE):
            x = jnp.concatenate([src[0, :, i] for src in srcs], axis=0)
            xp = (x + pos_ref[half * CMP_STRIDE + i][None]).reshape(m, dh)
            lhs_ref[half, :, (2 * i) * dh:(2 * i + 1) * dh] = jnp.where(is_k, xp, 0.0).astype(BF16)
            lhs_ref[half, :, (2 * i + 1) * dh:(2 * i + 2) * dh] = jnp.where(is_k, 0.0, xp).astype(BF16)
    a = jnp.dot(lhs_ref[0], w1_ref[0], preferred_element_type=F32)
    bm = jnp.dot(lhs_ref[1], w1_ref[1], preferred_element_type=F32)
    a_shift = jnp.concatenate([carry_ref[...], a[:m - nq]], axis=0)
    carry_ref[...] = a[m - nq:]
    hid = jax.nn.gelu(a_shift + bm)
    is_k_h = (lax.broadcasted_iota(jnp.int32, (m, hidden), 0) & (nq - 1)) < groups
    lhs2 = jnp.concatenate([jnp.where(is_k_h, hid, 0.0), jnp.where(is_k_h, 0.0, hid)], axis=1).astype(BF16)
    h_ref[...] = jnp.dot(lhs2, w2_ref[...], preferred_element_type=F32)
    for q in range(nq):
        o_ref[0, q // groups, q % groups] = h_ref[pl.ds(q, n, stride=nq), :].astype(o_ref.dtype)


def _compress_pages(cache_kv, page_table, cmp_pos, w_cmp1, w_cmp2, *, pages_per_step=8):
    n_pool, page, n_slots, groups, dh = cache_kv.shape
    batch, n_pages = page_table.shape
    hidden = w_cmp1.shape[-1]
    nq = 2 * groups
    assert nq == SUBLANES and page % CMP_STRIDE == 0
    subs = page // CMP_STRIDE
    halves = CMP_LEN // CMP_STRIDE
    pool = cache_kv.reshape(n_pool, subs, CMP_STRIDE, n_slots * groups, dh)
    pps = _pick(n_pages, pages_per_step)
    n = pps * subs
    m = n * nq
    pos = jnp.repeat(cmp_pos, groups, axis=0).transpose(1, 0, 2)
    w1 = (w_cmp1.reshape(2, halves, CMP_STRIDE, dh, hidden).transpose(1, 2, 0, 3, 4)
          .reshape(halves, CMP_STRIDE * 2 * dh, hidden).astype(BF16))
    w2 = w_cmp2.reshape(2 * hidden, dh).astype(BF16)

    def page_spec(r):
        return pl.BlockSpec((1, subs, CMP_STRIDE, nq, dh),
                            lambda b, j, pt: (pt[b * n_pages + j * pps + r], 0, 0, 0, 0))

    return pl.pallas_call(
        functools.partial(_compress_pages_body, pps=pps, groups=groups, dh=dh),
        grid_spec=pltpu.PrefetchScalarGridSpec(
            num_scalar_prefetch=1, grid=(batch, n_pages // pps),
            in_specs=[page_spec(r) for r in range(pps)] + [
                pl.BlockSpec((CMP_LEN, nq, dh), lambda b, j, pt: (0, 0, 0)),
                pl.BlockSpec((halves, CMP_STRIDE * 2 * dh, hidden), lambda b, j, pt: (0, 0, 0)),
                pl.BlockSpec((2 * hidden, dh), lambda b, j, pt: (0, 0))],
            out_specs=pl.BlockSpec((1, 2, groups, n, dh), lambda b, j, pt: (b, 0, 0, j, 0)),
            scratch_shapes=[pltpu.VMEM((nq, hidden), F32),
                            pltpu.VMEM((halves, m, CMP_STRIDE * 2 * dh), BF16),
                            pltpu.VMEM((m, dh), F32)]),
        out_shape=jax.ShapeDtypeStruct((batch, 2, groups, n_pages * subs, dh), BF16),
        compiler_params=_params("parallel", "arbitrary"),
        name="compress_pages",
    )(page_table.reshape(-1), *([pool] * pps), pos, w1, w2)


def _rel_bucket(dist):
    n = jnp.maximum(dist, 0)
    exact = REL_BUCKETS // 2
    nf = jnp.maximum(n, 1).astype(F32)
    large = exact + (jnp.log(nf / exact) / math.log(REL_MAX_DIST / exact) * (REL_BUCKETS - exact)).astype(jnp.int32)
    large = jnp.minimum(large, REL_BUCKETS - 1)
    return jnp.where(n < exact, n, large)


def _bias_tables_body(rb_ref, btab_ref, ec_ref):
    h = pl.program_id(0)
    t = KEY_TILE

    def bias_of(dist):
        bucket = _rel_bucket(dist)
        out = jnp.zeros(dist.shape, F32)
        for b in range(REL_BUCKETS):
            out = jnp.where(bucket == b, rb_ref[b, h], out)
        return out

    kj = lax.broadcasted_iota(jnp.int32, (t, t), 0)
    qi = lax.broadcasted_iota(jnp.int32, (t, t), 1)
    d0 = qi - kj
    far = bias_of(2 * t + d0)
    btab_ref[0, 0] = jnp.where(d0 >= 0, bias_of(d0), NEG_INF)
    btab_ref[0, 1] = bias_of(t + d0)
    btab_ref[0, 2] = far
    btab_ref[0, 3] = jnp.where(d0 <= 0, far, NEG_INF)
    off = lax.broadcasted_iota(jnp.int32, (2 * t, t), 0) - t
    qi2 = lax.broadcasted_iota(jnp.int32, (2 * t, t), 1)
    ec_ref[0] = bias_of(qi2 - CMP_STRIDE * off - (CMP_LEN - 1 - CMP_STRIDE))


def _bias_tables(rel_bias):
    heads = rel_bias.shape[1]
    t = KEY_TILE
    assert 2 * t - (t - 1) >= REL_MAX_DIST
    return pl.pallas_call(
        _bias_tables_body,
        grid=(heads,),
        in_specs=[pl.BlockSpec(memory_space=pltpu.SMEM)],
        out_specs=[pl.BlockSpec((1, 4, t, t), lambda h: (h, 0, 0, 0)),
                   pl.BlockSpec((1, 2 * t, t), lambda h: (h, 0, 0))],
        out_shape=[jax.ShapeDtypeStruct((heads, 4, t, t), F32),
                   jax.ShapeDtypeStruct((heads, 2 * t, t), F32)],
        compiler_params=_params("parallel"),
        name="rel_bias_tables",
    )(rel_bias)


def _head_bias(dist, tbl_ref):
    bucket = _rel_bucket(dist)
    r = tbl_ref.shape[0]
    out = jnp.zeros((r, dist.shape[1]), F32)
    for b in range(REL_BUCKETS):
        out = jnp.where(bucket == b, tbl_ref[:, b:b + 1], out)
    return out


def _nsa_prompt_body(q_ref, gate_ref, kc_ref, vc_ref, ks_ref, vs_ref, kw_ref, vw_ref, btab_ref, ec_ref,
                     o_ref, selb_ref, imp_ref, m_ref, l_ref, acc_ref, *, rep, dh, n_top, nblk):
    tq = KEY_TILE
    ns = kc_ref.shape[3]
    qt = pl.program_id(2)
    qs = qt * tq
    q = jnp.concatenate([q_ref[0, :, r * dh:(r + 1) * dh] for r in range(rep)], axis=0)
    nt = (((1,), (1,)), ((), ()))
    tn = (((0,), (0,)), ((), ()))

    def per_head(tile_of):
        return jnp.concatenate([tile_of(r) for r in range(rep)], axis=1)

    srow = lax.broadcasted_iota(jnp.int32, (ns, tq), 0)
    qi = lax.broadcasted_iota(jnp.int32, (ns, tq), 1)
    dist_c = qs + qi - (CMP_STRIDE * srow + (CMP_LEN - 1 - CMP_STRIDE))
    valid_c = (dist_c >= 0) & (srow >= 1)
    neg_c = jnp.where(valid_c, 0.0, NEG_INF)
    ec_start = pl.multiple_of(ec_ref.shape[1] // 2 - _shr(qs, CMP_STRIDE), SUBLANES)
    s_c = lax.dot_general(kc_ref[0, 0, 0], q, nt, preferred_element_type=F32)
    s_c = s_c + per_head(lambda r: ec_ref[r, pl.ds(ec_start, ns), :] + neg_c)
    m_c = jnp.max(s_c, axis=0, keepdims=True)
    e_c = jnp.where(per_head(lambda r: valid_c), jnp.exp(s_c - m_c), 0.0)
    p_c = e_c / jnp.maximum(jnp.sum(e_c, axis=0, keepdims=True), 1e-30)
    o_c = lax.dot_general(vc_ref[0, 0, 0], p_c.astype(BF16), tn, preferred_element_type=F32)

    ratio = SEL_LEN // CMP_STRIDE
    psum = p_c[:, 0:tq]
    for r in range(1, rep):
        psum = psum + p_c[:, r * tq:(r + 1) * tq]

    def ahead(kk):
        return jnp.where(srow + kk < ns, pltpu.roll(psum, ns - kk, 0), 0.0)

    imp_ref[...] = ahead(1) + 2.0 * (ahead(2) + ahead(3) + ahead(4)) + ahead(5)
    imp = imp_ref[pl.ds(0, nblk, stride=ratio), :]
    jb = lax.broadcasted_iota(jnp.int32, (nblk, tq), 0)
    cur = _shr(qs + lax.broadcasted_iota(jnp.int32, (nblk, tq), 1), SEL_LEN)
    forced = (jb == 0) | (jb == cur) | (jb == cur - 1)
    elig = jb <= cur
    score = jnp.where(forced, 1e30, jnp.where(elig, imp, -1.0))
    rank = jnp.zeros((nblk, tq), F32)
    for j2 in range(nblk):
        rowv = score[j2:j2 + 1, :]
        ahead_of = (rowv > score) | ((rowv == score) & (j2 < jb))
        rank = rank + jnp.where(ahead_of, 1.0, 0.0)
    drop = jnp.where((rank < n_top) & elig, 0.0, NEG_INF)
    krow_blk = _shr(lax.broadcasted_iota(jnp.int32, (KEY_TILE, tq), 0), SEL_LEN)
    blocks_per_tile = KEY_TILE // SEL_LEN
    for kt in range(selb_ref.shape[0]):
        tile = None
        for u in reversed(range(blocks_per_tile)):
            jbu = blocks_per_tile * kt + u
            rowu = jnp.broadcast_to(drop[jbu:jbu + 1, :], (KEY_TILE, tq))
            tile = rowu if tile is None else jnp.where(krow_blk == u, rowu, tile)
        selb_ref[kt] = tile

    def attend(k_ref, v_ref, lo, hi, tile_bias):
        m_ref[...] = jnp.full(m_ref.shape, NEG_INF, F32)
        l_ref[...] = jnp.zeros_like(l_ref)
        acc_ref[...] = jnp.zeros_like(acc_ref)

        def body(kt, carry):
            start = pl.multiple_of(kt * KEY_TILE, KEY_TILE)
            kk = k_ref[0, pl.ds(start, KEY_TILE), :].astype(BF16)
            vv = v_ref[0, pl.ds(start, KEY_TILE), :].astype(BF16)
            sc = lax.dot_general(kk, q, nt, preferred_element_type=F32) + tile_bias(kt)
            m_old = m_ref[...]
            m_new = jnp.maximum(m_old, jnp.max(sc, axis=0, keepdims=True))
            alpha = jnp.exp(m_old - m_new)
            pe = jnp.exp(sc - m_new)
            l_ref[...] = alpha * l_ref[...] + jnp.sum(pe, axis=0, keepdims=True)
            acc_ref[...] = alpha * acc_ref[...] + lax.dot_general(vv, pe.astype(BF16), tn,
                                                                  preferred_element_type=F32)
            m_ref[...] = m_new
            return carry

        lax.fori_loop(lo, hi, body, 0)
        return acc_ref[...] * (1.0 / l_ref[...])

    def sel_bias(kt):
        idx = jnp.minimum(qt - kt, 2)
        return per_head(lambda r: btab_ref[r, idx] + selb_ref[kt])

    def win_bias(kt):
        delta = qt - kt
        idx = jnp.where(delta == WINDOW // KEY_TILE, 3, jnp.minimum(delta, 2))
        return per_head(lambda r: btab_ref[r, idx])

    o_s = attend(ks_ref, vs_ref, 0, qt + 1, sel_bias)
    o_w = attend(kw_ref, vw_ref, jnp.maximum(qt - WINDOW // KEY_TILE, 0), qt + 1, win_bias)

    gates = gate_ref[0].T
    for r in range(rep):
        cols = slice(r * tq, (r + 1) * tq)
        o_r = (gates[r:r + 1] * o_c[:, cols] + gates[rep + r:rep + r + 1] * o_s[:, cols]
               + gates[2 * rep + r:2 * rep + r + 1] * o_w[:, cols])
        o_ref[0, :, r * dh:(r + 1) * dh] = o_r.T.astype(o_ref.dtype)


def _nsa_prompt(q, gates, cmp_kv, rows4, win2, btab, ectab, *, batch, seq, groups, rep, dh):
    tq = KEY_TILE
    assert seq % tq == 0 and WINDOW % KEY_TILE == 0 and KEY_TILE % SEL_LEN == 0
    ns = seq // CMP_STRIDE
    assert ns % LANES == 0 and ns == LANES
    nblk = -(-seq // SEL_LEN)
    n_top = min(SEL_TOP, nblk)
    nq = seq // tq
    nkt = seq // KEY_TILE
    return pl.pallas_call(
        functools.partial(_nsa_prompt_body, rep=rep, dh=dh, n_top=n_top, nblk=nblk),
        grid=(batch, groups, nq),
        in_specs=[pl.BlockSpec((1, tq, rep * dh), lambda b, g, i: (b, i, g)),
                  pl.BlockSpec((1, tq, LANES), lambda b, g, i: (b, i, g)),
                  pl.BlockSpec((1, 1, 1, ns, dh), lambda b, g, i: (b, 0, g, 0, 0)),
                  pl.BlockSpec((1, 1, 1, ns, dh), lambda b, g, i: (b, 1, g, 0, 0)),
                  pl.BlockSpec((1, seq, dh), lambda b, g, i: (b, 0, 2 * groups + g)),
                  pl.BlockSpec((1, seq, dh), lambda b, g, i: (b, 0, 3 * groups + g)),
                  pl.BlockSpec((1, seq, dh), lambda b, g, i: (b, 0, g)),
                  pl.BlockSpec((1, seq, dh), lambda b, g, i: (b, 0, groups + g)),
                  pl.BlockSpec((rep, 4, tq, KEY_TILE), lambda b, g, i: (g, 0, 0, 0)),
                  pl.BlockSpec((rep, 2 * tq, tq), lambda b, g, i: (g, 0, 0))],
        out_specs=pl.BlockSpec((1, tq, rep * dh), lambda b, g, i: (b, i, g)),
        out_shape=jax.ShapeDtypeStruct((batch, seq, groups * rep * dh), BF16),
        scratch_shapes=[pltpu.VMEM((nkt, KEY_TILE, tq), F32),
                        pltpu.VMEM((ns, tq), F32),
                        pltpu.VMEM((1, rep * tq), F32),
                        pltpu.VMEM((1, rep * tq), F32),
                        pltpu.VMEM((dh, rep * tq), F32)],
        compiler_params=_params("parallel", "parallel", "arbitrary"),
        name="nsa_prompt",
    )(q, gates, cmp_kv, cmp_kv, rows4, rows4, win2, win2, btab, ectab)


def _softmax_rows(s, valid):
    s = jnp.where(valid, s, NEG_INF)
    m = jnp.max(s, axis=-1, keepdims=True)
    e = jnp.where(valid, jnp.exp(s - m), 0.0)
    return e / jnp.maximum(jnp.sum(e, axis=-1, keepdims=True), 1e-30)


def _nsa_sample_cmp_body(q_ref, kc_ref, vc_ref, tbl_ref, oc_ref, ids_ref, *, t_pos, n_sel):
    ns = kc_ref.shape[3]
    rep = q_ref.shape[1]
    q = q_ref[0].astype(BF16)
    lane = lax.broadcasted_iota(jnp.int32, (1, ns), 1)
    dist_c = t_pos - (CMP_STRIDE * lane + (CMP_LEN - 1 - CMP_STRIDE))
    valid = (dist_c >= 0) & (lane >= 1)
    s = lax.dot_general(q, kc_ref[0, 0, 0], (((1,), (1,)), ((), ())), preferred_element_type=F32)
    p = _softmax_rows(s + _head_bias(dist_c, tbl_ref), valid)
    oc_ref[0, 0] = jnp.dot(p.astype(BF16), vc_ref[0, 0, 0], preferred_element_type=F32)

    ratio = SEL_LEN // CMP_STRIDE
    psum = jnp.broadcast_to(jnp.sum(p, axis=0, keepdims=True), (rep, ns))
    lane_r = lax.broadcasted_iota(jnp.int32, (rep, ns), 1)

    def ahead(kk):
        return jnp.where(lane_r + kk < ns, pltpu.roll(psum, ns - kk, 1), 0.0)

    imp = ahead(1) + 2.0 * (ahead(2) + ahead(3) + ahead(4)) + ahead(5)
    jb = _shr(lane_r, ratio)
    cur = t_pos // SEL_LEN
    forced = (jb == 0) | (jb == cur - 1)
    cand = ((lane_r & (ratio - 1)) == 0) & (jb < cur)
    score = jnp.where(cand, jnp.where(forced, 1e30, imp), -1.0)
    lane_f = lane_r.astype(F32)
    lane_o = lax.broadcasted_iota(jnp.int32, (rep, LANES), 1)
    ids = jnp.full((rep, LANES), cur, jnp.int32)
    for n in range(n_sel):
        best = jnp.max(score, axis=-1, keepdims=True)
        first = jnp.min(jnp.where(score == best, lane_f, float(ns)), axis=-1, keepdims=True)
        ids = jnp.where(lane_o == n, _shr(first.astype(jnp.int32), ratio), ids)
        score = jnp.where(lane_f == first, -2.0, score)
    ids_ref[0, 0] = ids


def _nsa_sample_sel_body(pt_ref, ids_ref, q_ref, kv_ref, new_ref, win_ref, oc_ref, gate_ref, tbl_ref,
                         o_ref, m_ref, l_ref, acc_ref, *, t_pos, groups, n_sel, w_start):
    b = pl.program_id(0)
    g = pl.program_id(1)
    n = pl.program_id(2)
    rep, dh = q_ref.shape[1], q_ref.shape[2]
    nq = 2 * groups
    nt = (((1,), (1,)), ((), ()))
    q = q_ref[0].astype(BF16)
    cur = t_pos // SEL_LEN

    def own_rows(n_rows, first_dist):
        col = lax.broadcasted_iota(jnp.int32, (1, n_rows * nq), 1)
        return jnp.where((col & (nq - 1)) == g, first_dist - _shr(col, nq), -1)

    def new_row(slot):
        return new_ref[0, pl.ds(slot * groups + g, 1), :].astype(BF16)

    def self_score(k_new):
        s0 = jnp.sum(q.astype(F32) * k_new.astype(F32), axis=-1, keepdims=True)
        return s0 + _head_bias(jnp.zeros((1, 1), jnp.int32), tbl_ref)

    @pl.when(n == 0)
    def _():
        m_ref[...] = self_score(new_row(2))
        l_ref[...] = jnp.ones_like(l_ref)
        acc_ref[...] = jnp.broadcast_to(new_row(3).astype(F32), acc_ref.shape)

    blk = ids_ref[(b * groups + g) * LANES + n]
    kv = kv_ref[0, 0].reshape(SEL_LEN * nq, dh).astype(BF16)
    dist = own_rows(SEL_LEN, jnp.where(blk < cur, t_pos - blk * SEL_LEN, -1))
    valid = dist >= 0
    sc = lax.dot_general(q, kv, nt, preferred_element_type=F32) + _head_bias(dist, tbl_ref)
    sc = jnp.where(valid, sc, NEG_INF)
    m_old = m_ref[...]
    m_new = jnp.maximum(m_old, jnp.max(sc, axis=-1, keepdims=True))
    alpha = jnp.exp(m_old - m_new)
    pe = jnp.where(valid, jnp.exp(sc - m_new), 0.0)
    l_new = alpha * l_ref[...] + jnp.sum(pe, axis=-1, keepdims=True)
    acc_new = alpha * acc_ref[...] + jnp.dot(pltpu.roll(pe, groups, 1).astype(BF16), kv,
                                             preferred_element_type=F32)
    m_ref[...] = m_new
    l_ref[...] = l_new
    acc_ref[...] = acc_new

    @pl.when(n == n_sel - 1)
    def _():
        o_s = acc_new / l_new
        nw = win_ref.shape[1]
        win = win_ref[0].reshape(nw * nq, dh).astype(BF16)
        dist_w = own_rows(nw, t_pos - w_start)
        valid_w = (dist_w >= 0) & (dist_w <= WINDOW)
        sw = lax.dot_general(q, win, nt, preferred_element_type=F32)
        sw = jnp.where(valid_w, sw + _head_bias(dist_w, tbl_ref), NEG_INF)
        s_self = self_score(new_row(4))
        mw = jnp.maximum(jnp.max(sw, axis=-1, keepdims=True), s_self)
        ew = jnp.where(valid_w, jnp.exp(sw - mw), 0.0)
        e_self = jnp.exp(s_self - mw)
        lw = jnp.sum(ew, axis=-1, keepdims=True) + e_self
        pw = ew / lw
        p_self = e_self / lw
        o_w = (jnp.dot(pltpu.roll(pw, groups, 1).astype(BF16), win, preferred_element_type=F32)
               + p_self.astype(BF16).astype(F32) * new_row(5).astype(F32))
        gates = gate_ref[0, 0]
        o_ref[0, 0] = (gates[:, 0:1] * oc_ref[0, 0] + gates[:, 1:2] * o_s + gates[:, 2:3] * o_w).astype(o_ref.dtype)


def _nsa_sample(q, gates, cmp_kv, kv_new, cache_kv, cache_win, page_table, rel_bias, *, groups, rep, dh, past_len):
    batch = q.shape[0]
    n_pool, page = cache_kv.shape[0], cache_kv.shape[1]
    n_pages = page_table.shape[1]
    t_pos = past_len
    assert past_len % SEL_LEN == 0 and page % SEL_LEN == 0
    ns = past_len // CMP_STRIDE
    assert ns % LANES == 0
    nblk = past_len // SEL_LEN + 1
    n_sel = min(SEL_TOP, nblk) - 1
    assert 1 <= n_sel <= LANES
    tbl = rel_bias.T
    o_c, ids = pl.pallas_call(
        functools.partial(_nsa_sample_cmp_body, t_pos=t_pos, n_sel=n_sel),
        grid=(batch, groups),
        in_specs=[pl.BlockSpec((1, rep, dh), lambda b, g: (b, g, 0)),
                  pl.BlockSpec((1, 1, 1, ns, dh), lambda b, g: (b, 0, g, 0, 0)),
                  pl.BlockSpec((1, 1, 1, ns, dh), lambda b, g: (b, 1, g, 0, 0)),
                  pl.BlockSpec((rep, REL_BUCKETS), lambda b, g: (g, 0))],
        out_specs=[pl.BlockSpec((1, 1, rep, dh), lambda b, g: (b, g, 0, 0)),
                   pl.BlockSpec((1, 1, rep, LANES), lambda b, g: (b, g, 0, 0))],
        out_shape=[jax.ShapeDtypeStruct((batch, groups, rep, dh), F32),
                   jax.ShapeDtypeStruct((batch, groups, rep, LANES), jnp.int32)],
        compiler_params=_params("parallel", "parallel"),
        name="nsa_sample_compressed",
    )(q, cmp_kv, cmp_kv, tbl)

    halves = page // SEL_LEN
    nq = 2 * groups
    assert nq == SUBLANES and cache_kv.shape[2] == 4 and cache_win.shape[2] == 2
    pool = cache_kv.reshape(n_pool, halves, SEL_LEN, 2 * nq, dh)
    win = cache_win.reshape(batch, cache_win.shape[1], nq, dh)
    w_start = past_len - cache_win.shape[1]
    ids_flat = ids[:, :, 0, :].reshape(-1)

    def blk_map(b, g, n, pt, sel):
        blk = jnp.minimum(sel[(b * groups + g) * LANES + n], nblk - 2)
        return (pt[b * n_pages + blk // halves], blk % halves, 0, 1, 0)

    out = pl.pallas_call(
        functools.partial(_nsa_sample_sel_body, t_pos=t_pos, groups=groups, n_sel=n_sel, w_start=w_start),
        grid_spec=pltpu.PrefetchScalarGridSpec(
            num_scalar_prefetch=2, grid=(batch, groups, n_sel),
            in_specs=[pl.BlockSpec((1, rep, dh), lambda b, g, n, pt, sel: (b, g, 0)),
                      pl.BlockSpec((1, 1, SEL_LEN, nq, dh), blk_map),
                      pl.BlockSpec((1, 6 * groups, dh), lambda b, g, n, pt, sel: (b, 0, 0)),
                      pl.BlockSpec((1, win.shape[1], nq, dh), lambda b, g, n, pt, sel: (b, 0, 0, 0)),
                      pl.BlockSpec((1, 1, rep, dh), lambda b, g, n, pt, sel: (b, g, 0, 0)),
                      pl.BlockSpec((1, 1, rep, 3), lambda b, g, n, pt, sel: (b, g, 0, 0)),
                      pl.BlockSpec((rep, REL_BUCKETS), lambda b, g, n, pt, sel: (g, 0))],
            out_specs=pl.BlockSpec((1, 1, rep, dh), lambda b, g, n, pt, sel: (b, g, 0, 0)),
            scratch_shapes=[pltpu.VMEM((rep, 1), F32), pltpu.VMEM((rep, 1), F32), pltpu.VMEM((rep, dh), F32)]),
        out_shape=jax.ShapeDtypeStruct((batch, groups, rep, dh), F32),
        compiler_params=_params("parallel", "parallel", "arbitrary"),
        name="nsa_sample_selected_window",
    )(page_table.reshape(-1), ids_flat, q, pool, kv_new, win, o_c, gates, tbl)
    return out


def _rotary_tables(pos, dk):
    half = dk // 2
    inv = ROPE_BASE ** (-jnp.arange(half, dtype=F32) / half)
    ang = pos.astype(F32)[:, None] * inv[None, :]
    return jnp.cos(ang), jnp.sin(ang)


def _mlp_block(x32, x16, w1, w2, layer, g, b, alpha, *, want_bf16, tag):
    hid = _matmul(x16, w1, w1.shape[2], layer=layer, out_dtype=BF16, epilogue=_ep_relu2, name=tag + "_ff1")
    y = _matmul(hid, w2, w2.shape[2], layer=layer, name=tag + "_ff2")
    return _add_ln(x32, y, g, b, alpha, want_bf16=want_bf16, name=tag + "_ln_ff")


def kernel(x_prompt, x_sample, state_ret, cache_kv, cache_win, page_table, w_ret_in, ret_gn_g, w_ret_out, w_kv,
           cmp_pos, w_cmp1, w_cmp2, w_qg, w_nsa_out, rel_bias, w_ff1, w_ff2, ln_g, ln_b):
    bsz, seq, d = x_prompt.shape
    dbsz, dseq, _ = x_sample.shape
    assert dseq == 1
    n_a, _, ret_heads, dk, dv = state_ret.shape
    depth = ln_g.shape[0]
    assert n_a == 1 and depth == 2
    groups, dh = cache_kv.shape[3], cache_kv.shape[4]
    heads = rel_bias.shape[1]
    rep = heads // groups
    past_len = page_table.shape[1] * cache_kv.shape[1]
    alpha = (2.0 * depth) ** 0.25
    hkv = groups * dh
    assert rep * 3 <= LANES and ret_heads * dk == d and heads * dh == d

    lg = jnp.log1p(-jnp.exp2(-5.0 - jnp.arange(ret_heads, dtype=F32)))
    lg_tab = jnp.broadcast_to(lg[:, None, None], (ret_heads, 1, LANES))
    gn_g = ret_gn_g[0].reshape(1, ret_heads * dv)
    w_in, w_out = w_ret_in, w_ret_out
    w_q, w_o = w_qg, w_nsa_out
    w_gate_raw = w_qg[0, :, heads * dh:]
    w_gate_p = jnp.pad(w_gate_raw.reshape(d, groups, rep, 3).transpose(0, 1, 3, 2).reshape(d, groups, 3 * rep),
                       ((0, 0), (0, 0), (0, LANES - 3 * rep))).reshape(d, groups * LANES)
    w_gate_s = jnp.pad(w_gate_raw, ((0, 0), (0, (-3 * heads) % LANES)))

    def trunk(x, pos0, is_prompt):
        b, t, _ = x.shape
        m = b * t
        x32 = x.reshape(m, d)
        x16 = x32.astype(BF16)
        pos = pos0 + jnp.arange(t)
        cos, sin = _rotary_tables(pos, dk)
        tag = "p" if is_prompt else "s"
        tm = _pick(m, 2048)
        rows_per_seq_blocks = max(t // tm, 1)
        if t >= tm:
            rot_specs = [pl.BlockSpec((tm, dk // 2), lambda i, j, k: (i % rows_per_seq_blocks, 0))] * 2
            rot_tabs = (cos, sin)
        else:
            rot_tabs = (jnp.tile(cos, (b, 1)), jnp.tile(sin, (b, 1)))
            rot_specs = [pl.BlockSpec((tm, dk // 2), lambda i, j, k: (i, 0))] * 2
        act = BF16 if is_prompt else F32
        qk = _matmul(x16, w_in, 2 * d, out_dtype=act, extra=rot_tabs, extra_specs=rot_specs,
                     epilogue=functools.partial(_ep_rotary, head_dim=dk, k_col_start=d, k_scale=dk ** -0.5),
                     name=tag + "_ret_qk")
        v = _matmul(x16, w_in, 2 * d, col_off=2 * d, out_dtype=act, name=tag + "_ret_v")
        gate = _matmul(x16, w_in, 2 * d, col_off=4 * d, name=tag + "_ret_gate")
        if is_prompt:
            o, st = _retention_prompt(qk.reshape(b, t, 2 * d), v.reshape(b, t, 2 * d), gate.reshape(b, t, 2 * d),
                                      gn_g, lg_tab, heads=ret_heads, dk=dk, dv=dv)
            o = o.reshape(m, 2 * d)
        else:
            q = qk[:, :d].reshape(b * ret_heads, 1, dk)
            k = qk[:, d:].reshape(b * ret_heads, 1, dk)
            o, st = _retention_sample(q, k, v.reshape(b * ret_heads, 1, dv), gate.reshape(b * ret_heads, 1, dv),
                                      gn_g, lg_tab, state_ret[0], heads=ret_heads, dk=dk, dv=dv)
            o = o.reshape(m, 2 * d)
        a = _matmul(o, w_out, d, name=tag + "_ret_out")
        x32, x16 = _add_ln(x32, a, ln_g[0, 0], ln_b[0, 0], alpha, name=tag + "_ln_ret")
        x32, x16 = _mlp_block(x32, x16, w_ff1, w_ff2, 0, ln_g[0, 1], ln_b[0, 1], alpha, want_bf16=True, tag=tag + "0")
        rows4 = _matmul(x16, w_kv, 4 * hkv, name=tag + "_kv_rows")
        win2 = _matmul(x16, w_kv, 2 * hkv, col_off=4 * hkv, name=tag + "_kv_win")
        q = _matmul(x16, w_q, heads * dh, out_dtype=act, epilogue=functools.partial(_ep_scale, scale=dh ** -0.5),
                    name=tag + "_nsa_q")
        if is_prompt:
            gates = _matmul(x16, w_gate_p, groups * LANES, epilogue=_ep_sigmoid, name=tag + "_nsa_gates")
            cmp_kv = _compress_prompt(rows4.reshape(b, t, 4 * hkv), cmp_pos, w_cmp1, w_cmp2,
                                      batch=b, groups=groups, dh=dh, seq=t)
            btab, ectab = _bias_tables(rel_bias)
            o = _nsa_prompt(q.reshape(b, t, heads * dh), gates.reshape(b, t, groups * LANES), cmp_kv,
                            rows4.reshape(b, t, 4 * hkv), win2.reshape(b, t, 2 * hkv), btab, ectab,
                            batch=b, seq=t, groups=groups, rep=rep, dh=dh)
            o = o.reshape(m, heads * dh)
            wp = min(WINDOW, t)
            new_rows = (rows4.reshape(b, t, 4, groups, dh), win2.reshape(b, t, 2, groups, dh)[:, t - wp:])
        else:
            gates = _matmul(x16, w_gate_s, w_gate_s.shape[1], epilogue=_ep_sigmoid, name=tag + "_nsa_gates")
            gates = gates[:, :3 * heads].reshape(b, groups, rep, 3)
            cmp_kv = _compress_pages(cache_kv, page_table, cmp_pos, w_cmp1, w_cmp2)
            kv_new = jnp.concatenate([rows4, win2], axis=1).reshape(b, 6 * groups, dh)
            o = _nsa_sample(q.reshape(b, heads, dh), gates, cmp_kv, kv_new, cache_kv, cache_win, page_table, rel_bias,
                            groups=groups, rep=rep, dh=dh, past_len=past_len)
            o = o.reshape(m, heads * dh)
            new_rows = (rows4.reshape(b, t, 4, groups, dh), win2.reshape(b, t, 2, groups, dh))
        a = _matmul(o, w_o, d, name=tag + "_nsa_out")
        x32, x16 = _add_ln(x32, a, ln_g[1, 0], ln_b[1, 0], alpha, name=tag + "_ln_nsa")
        y, _ = _mlp_block(x32, x16, w_ff1, w_ff2, 1, ln_g[1, 1], ln_b[1, 1], alpha, want_bf16=False, tag=tag + "1")
        return y.reshape(b, t, d), st[None], new_rows

    y_p, st_p, (kv_p, win_p) = trunk(x_prompt, 0, True)
    y_s, st_s, (kv_s, win_s) = trunk(x_sample, past_len, False)
    return (y_p, y_s, st_p, kv_p, win_p, st_s, kv_s, win_s)
```
